```python
import math
import jax, jax.numpy as jnp
from jax import lax
import numpy as np

D_MODEL = 1024
BATCH = 8
SEQ = 2048
DEPTH = 4
DEC_BATCH = 128
DEC_SEQ = 8
PAST_LEN = 16384
PAGE_SIZE = 128

D_SSM = D_MODEL // 2
SSM_GROUP = 16
SSM_GROUPS = D_SSM // SSM_GROUP
SSM_STATE = 64
RET_HEADS = 4
RET_DK = D_MODEL // (2 * RET_HEADS)
RET_DV = 2 * RET_DK
RET_CHUNK = 128
ROPE_BASE = 10000.0
D_FF = ((8 * D_MODEL // 3 + 127) // 128) * 128
CONV_W = 3
EPS = 1e-6
QK_W = RET_HEADS * RET_DK
V_W = RET_HEADS * RET_DV
IN_COLS = D_SSM + 2 * QK_W + 2 * V_W + 2 * D_MODEL
SPLITS = (D_SSM, D_SSM + QK_W, D_SSM + 2 * QK_W, D_SSM + 2 * QK_W + V_W,
          D_SSM + 2 * QK_W + 2 * V_W, D_SSM + 2 * QK_W + 2 * V_W + D_MODEL)

kernel_name = "hybrid_s5_retention_convffn_step"


def rmsnorm(x, g):
    xf = x.astype(jnp.float32)
    y = xf * lax.rsqrt(jnp.mean(xf * xf, axis=-1, keepdims=True) + EPS)
    return (y * g.astype(jnp.float32)).astype(x.dtype)


def rotary(x, pos):
    half = x.shape[-1] // 2
    inv = ROPE_BASE ** (-jnp.arange(half, dtype=jnp.float32) / half)
    ang = pos[:, None] * inv[None, :]
    cos = jnp.cos(ang)[None, :, None, :]
    sin = jnp.sin(ang)[None, :, None, :]
    xf = x.astype(jnp.float32)
    x1, x2 = xf[..., :half], xf[..., half:]
    return jnp.concatenate([x1 * cos - x2 * sin, x1 * sin + x2 * cos], axis=-1)


def s5_scan(u, s0, lam_re, lam_im, log_dt, b_re, b_im, c_re, c_im, d):
    bsz, L, _ = u.shape
    uf = u.astype(jnp.float32)
    ug = uf.reshape(bsz, L, SSM_GROUPS, SSM_GROUP)
    lam = lax.complex(lam_re.astype(jnp.float32), lam_im.astype(jnp.float32))
    dt = jnp.exp(log_dt.astype(jnp.float32))[:, None]
    lam_bar = jnp.exp(lam * dt)
    b = lax.complex(b_re.astype(jnp.float32), b_im.astype(jnp.float32))
    b_bar = ((lam_bar - 1.0) / lam)[..., None] * b
    bu = jnp.einsum('gpc,blgc->blgp', b_bar, ug.astype(jnp.complex64))
    bu = bu.at[:, 0].add(lam_bar[None] * s0)
    a = jnp.broadcast_to(lam_bar, bu.shape)

    def combine(e1, e2):
        a1, b1 = e1
        a2, b2 = e2
        return a1 * a2, a2 * b1 + b2

    _, s = lax.associative_scan(combine, (a, bu), axis=1)
    y = (jnp.einsum('gcp,blgp->blgc', c_re.astype(jnp.float32), jnp.real(s))
         - jnp.einsum('gcp,blgp->blgc', c_im.astype(jnp.float32), jnp.imag(s)))
    y = y.reshape(bsz, L, D_SSM) + d.astype(jnp.float32) * uf
    return y.astype(u.dtype), s[:, -1]


def retention(q, k, v, r0):
    bsz, L, H, dk = q.shape
    dv = v.shape[-1]
    c = RET_CHUNK if L % RET_CHUNK == 0 else L
    n = L // c
    log_g = jnp.log1p(-jnp.exp2(-5.0 - jnp.arange(H, dtype=jnp.float32)))
    idx = jnp.arange(c, dtype=jnp.float32)
    rel = idx[:, None] - idx[None, :]
    intra = jnp.where(rel >= 0, jnp.exp(jnp.maximum(rel, 0.0)[None] * log_g[:, None, None]), 0.0)
    inner = jnp.exp((idx[None, :] + 1.0) * log_g[:, None])
    tail = jnp.exp((c - 1.0 - idx[None, :]) * log_g[:, None])
    chunk_decay = jnp.exp(c * log_g)

    def to_chunks(t):
        return t.reshape(bsz, n, c, H, t.shape[-1]).transpose(1, 0, 3, 2, 4)

    qc, kc, vc = to_chunks(q), to_chunks(k), to_chunks(v.astype(jnp.float32))

    def step(r, inp):
        qi, ki, vi = inp
        sc = jnp.einsum('bhid,bhjd->bhij', qi, ki) * intra[None]
        o = (jnp.einsum('bhij,bhjv->bhiv', sc, vi)
             + jnp.einsum('bhid,bhdv->bhiv', qi, r) * inner[None, :, :, None])
        r_new = (r * chunk_decay[None, :, None, None]
                 + jnp.einsum('bhjd,bhjv->bhdv', ki * tail[None, :, :, None], vi))
        return r_new, o

    r, o = lax.scan(step, r0, (qc, kc, vc))
    o = o.transpose(1, 0, 3, 2, 4).reshape(bsz, L, H, dv)
    return o, r


def decoder_layer(x, pos, ssm_s0, ret_s0, conv_buf,
                  norm_mix, w_in, lam_re, lam_im, log_dt, b_re, b_im, c_re, c_im, d,
                  w_glu, w_ssm_out, w_ret_out, w_o, norm_ffn, w_up, conv_w, conv_b, w_down):
    bsz, L, _ = x.shape
    h = rmsnorm(x, norm_mix)
    z = h @ w_in
    u, q, k, v, g_ret, g_a, g_b = jnp.split(z, SPLITS, axis=-1)
    ya, ssm_last = s5_scan(u, ssm_s0, lam_re, lam_im, log_dt, b_re, b_im, c_re, c_im, d)
    ya = jax.nn.gelu(ya)
    ya = ya * jax.nn.sigmoid(ya @ w_glu)
    ya = ya @ w_ssm_out
    qr = rotary(q.reshape(bsz, L, RET_HEADS, RET_DK), pos)
    kr = rotary(k.reshape(bsz, L, RET_HEADS, RET_DK), pos) * (RET_DK ** -0.5)
    o, ret_last = retention(qr, kr, v.reshape(bsz, L, RET_HEADS, RET_DV), ret_s0)
    o = o * lax.rsqrt(jnp.mean(o * o, axis=-1, keepdims=True) + EPS)
    o = jax.nn.silu(g_ret) * o.reshape(bsz, L, V_W).astype(x.dtype)
    yb = o @ w_ret_out
    mix = jax.nn.sigmoid(g_a) * ya + jax.nn.sigmoid(g_b) * yb
    x = x + mix @ w_o
    h2 = rmsnorm(x, norm_ffn)
    up = h2 @ w_up
    hp = jnp.concatenate([conv_buf.astype(up.dtype), up], axis=1)
    hc = conv_b
    for j in range(CONV_W):
        hc = hc + conv_w[j] * hp[:, j:j + L]
    new_buf = hp[:, L:]
    val, gate = jnp.split(hc, 2, axis=-1)
    x = x + (jax.nn.silu(gate) * val) @ w_down
    return x, ssm_last, ret_last, new_buf


def setup_inputs(seed: int = 0) -> dict:
    key = jax.random.key(seed)
    ks = iter(jax.random.split(key, 32))

    def nrm(shape, scale):
        return scale * jax.random.normal(next(ks), shape, jnp.float32)

    L = DEPTH
    G, P = SSM_GROUPS, SSM_STATE
    x_prompt = nrm((BATCH, SEQ, D_MODEL), 1.0)
    x_sample = nrm((DEC_BATCH, DEC_SEQ, D_MODEL), 1.0)
    state_ssm_re = nrm((L, DEC_BATCH, G, P), 0.3)
    state_ssm_im = nrm((L, DEC_BATCH, G, P), 0.3)
    state_ret = nrm((L, DEC_BATCH, RET_HEADS, RET_DK, RET_DV), 0.5)
    state_conv = nrm((L, DEC_BATCH, CONV_W - 1, 2 * D_FF), 1.0)
    norm_mix = 1.0 + nrm((L, D_MODEL), 0.02)
    w_in = nrm((L, D_MODEL, IN_COLS), D_MODEL ** -0.5)
    ssm_lam_re = -0.5 + nrm((L, G, P), 0.01)
    ssm_lam_im = math.pi * jnp.arange(P, dtype=jnp.float32) + nrm((L, G, P), 0.01)
    ssm_log_dt = jax.random.uniform(next(ks), (L, G), jnp.float32, math.log(1e-3), math.log(1e-1))
    ssm_b_re = nrm((L, G, P, SSM_GROUP), (2 * SSM_GROUP) ** -0.5)
    ssm_b_im = nrm((L, G, P, SSM_GROUP), (2 * SSM_GROUP) ** -0.5)
    ssm_c_re = nrm((L, G, SSM_GROUP, P), (2 * P) ** -0.5)
    ssm_c_im = nrm((L, G, SSM_GROUP, P), (2 * P) ** -0.5)
    ssm_d = nrm((L, D_SSM), 0.5)
    w_glu = nrm((L, D_SSM, D_SSM), D_SSM ** -0.5)
    w_ssm_out = nrm((L, D_SSM, D_MODEL), D_SSM ** -0.5)
    w_ret_out = nrm((L, V_W, D_MODEL), V_W ** -0.5)
    w_o = nrm((L, D_MODEL, D_MODEL), D_MODEL ** -0.5)
    norm_ffn = 1.0 + nrm((L, D_MODEL), 0.02)
    w_up = nrm((L, D_MODEL, 2 * D_FF), D_MODEL ** -0.5)
    conv_w = nrm((L, CONV_W, 2 * D_FF), CONV_W ** -0.5)
    conv_b = nrm((L, 2 * D_FF), 0.02)
    w_down = nrm((L, D_FF, D_MODEL), D_FF ** -0.5)
    norm_final = 1.0 + nrm((D_MODEL,), 0.02)
    return {"x_prompt": x_prompt, "x_sample": x_sample,
            "state_ssm_re": state_ssm_re, "state_ssm_im": state_ssm_im,
            "state_ret": state_ret, "state_conv": state_conv,
            "norm_mix": norm_mix, "w_in": w_in,
            "ssm_lam_re": ssm_lam_re, "ssm_lam_im": ssm_lam_im, "ssm_log_dt": ssm_log_dt,
            "ssm_b_re": ssm_b_re, "ssm_b_im": ssm_b_im, "ssm_c_re": ssm_c_re, "ssm_c_im": ssm_c_im,
            "ssm_d": ssm_d, "w_glu": w_glu, "w_ssm_out": w_ssm_out, "w_ret_out": w_ret_out,
            "w_o": w_o, "norm_ffn": norm_ffn, "w_up": w_up, "conv_w": conv_w, "conv_b": conv_b,
            "w_down": w_down, "norm_final": norm_final}


def reference(x_prompt, x_sample, state_ssm_re, state_ssm_im, state_ret, state_conv,
              norm_mix, w_in, ssm_lam_re, ssm_lam_im, ssm_log_dt, ssm_b_re, ssm_b_im,
              ssm_c_re, ssm_c_im, ssm_d, w_glu, w_ssm_out, w_ret_out, w_o, norm_ffn,
              w_up, conv_w, conv_b, w_down, norm_final):
    pos_p = jnp.arange(SEQ, dtype=jnp.float32)
    pos_s = PAST_LEN + jnp.arange(DEC_SEQ, dtype=jnp.float32)
    bp = x_prompt.shape[0]
    xp, xs = x_prompt, x_sample
    ssm_re_p, ssm_im_p, ret_p, conv_p = [], [], [], []
    ssm_re_s, ssm_im_s, ret_s, conv_s = [], [], [], []
    for l in range(DEPTH):
        lw = (norm_mix[l], w_in[l], ssm_lam_re[l], ssm_lam_im[l], ssm_log_dt[l],
              ssm_b_re[l], ssm_b_im[l], ssm_c_re[l], ssm_c_im[l], ssm_d[l],
              w_glu[l], w_ssm_out[l], w_ret_out[l], w_o[l], norm_ffn[l],
              w_up[l], conv_w[l], conv_b[l], w_down[l])
        s0_p = jnp.zeros((bp, SSM_GROUPS, SSM_STATE), jnp.complex64)
        r0_p = jnp.zeros((bp, RET_HEADS, RET_DK, RET_DV), jnp.float32)
        c0_p = jnp.zeros((bp, CONV_W - 1, 2 * D_FF), xp.dtype)
        xp, sp, rp, cp = decoder_layer(xp, pos_p, s0_p, r0_p, c0_p, *lw)
        ssm_re_p.append(jnp.real(sp).astype(state_ssm_re.dtype))
        ssm_im_p.append(jnp.imag(sp).astype(state_ssm_im.dtype))
        ret_p.append(rp.astype(state_ret.dtype))
        conv_p.append(cp.astype(state_conv.dtype))
        s0_s = lax.complex(state_ssm_re[l].astype(jnp.float32), state_ssm_im[l].astype(jnp.float32))
        r0_s = state_ret[l].astype(jnp.float32)
        xs, ss, rs, cs = decoder_layer(xs, pos_s, s0_s, r0_s, state_conv[l], *lw)
        ssm_re_s.append(jnp.real(ss).astype(state_ssm_re.dtype))
        ssm_im_s.append(jnp.imag(ss).astype(state_ssm_im.dtype))
        ret_s.append(rs.astype(state_ret.dtype))
        conv_s.append(cs.astype(state_conv.dtype))
    y_prompt = rmsnorm(xp, norm_final)
    y_sample = rmsnorm(xs, norm_final)
    return (y_prompt, y_sample,
            jnp.stack(ssm_re_p), jnp.stack(ssm_im_p), jnp.stack(ret_p), jnp.stack(conv_p),
            jnp.stack(ssm_re_s), jnp.stack(ssm_im_s), jnp.stack(ret_s), jnp.stack(conv_s))
```

```python
import functools
import math

import jax
import jax.numpy as jnp
from jax import lax
from jax.experimental import pallas as pl
from jax.experimental.pallas import tpu as pltpu

D_MODEL = 1024
DEPTH = 4
PAST_LEN = 16384
D_SSM = D_MODEL // 2
SSM_GROUP = 16
SSM_GROUPS = D_SSM // SSM_GROUP
SSM_STATE = 64
RET_HEADS = 4
RET_DK = D_MODEL // (2 * RET_HEADS)
RET_DV = 2 * RET_DK
ROPE_BASE = 10000.0
D_FF = ((8 * D_MODEL // 3 + 127) // 128) * 128
CONV_W = 3
EPS = 1e-6
QK_W = RET_HEADS * RET_DK
V_W = RET_HEADS * RET_DV
IN_COLS = D_SSM + 2 * QK_W + 2 * V_W + 2 * D_MODEL
C_U, C_Q, C_K, C_V, C_GR, C_GA, C_GB = 0, D_SSM, D_SSM + QK_W, D_SSM + 2 * QK_W, \
    D_SSM + 2 * QK_W + V_W, D_SSM + 2 * QK_W + 2 * V_W, D_SSM + 2 * QK_W + 2 * V_W + D_MODEL

SUBLANES = 8
LANES = 128
SSM_BLK_CH = LANES
SSM_NBLK = D_SSM // SSM_BLK_CH
SSM_BLK_GROUPS = SSM_BLK_CH // SSM_GROUP
SSM_BLK_STATE = SSM_BLK_GROUPS * SSM_STATE
FF_TILE = 256
FF_NT = D_FF // FF_TILE
VMEM_LIMIT = 56 * 1024 * 1024

BF16 = jnp.bfloat16
F32 = jnp.float32


def _dot(a, b):
    return jnp.dot(a, b, preferred_element_type=F32)


def _rms(x, g):
    ms = jnp.mean(x * x, axis=-1, keepdims=True)
    return x * lax.rsqrt(ms + EPS) * g


def _const_spec(block_shape, index_map):
    return pl.BlockSpec(block_shape, index_map, pipeline_mode=pl.Buffered(1))


def _s5_prep_kernel(lr_ref, li_ref, ldt_ref, br_ref, bi_ref, ar_ref, ai_ref, bbr_ref, bbi_ref):
    lr = lr_ref[...]
    li = li_ref[...]
    dt = jnp.exp(ldt_ref[...])
    mag = jnp.exp(lr * dt)
    ar = mag * jnp.cos(li * dt)
    ai = mag * jnp.sin(li * dt)
    nr = ar - 1.0
    den = lr * lr + li * li
    fr = (nr * lr + ai * li) / den
    fi = (ai * lr - nr * li) / den
    br = br_ref[...]
    bi = bi_ref[...]
    ar_ref[...] = ar
    ai_ref[...] = ai
    bbr_ref[...] = fr * br - fi * bi
    bbi_ref[...] = fr * bi + fi * br


def _s5_prep(lam_re, lam_im, log_dt, b_re, b_im):
    depth = lam_re.shape[0]
    rows = SSM_GROUPS * SSM_GROUP
    rep = lambda a: jnp.repeat(a, SSM_GROUP, axis=1)
    lr = rep(lam_re)
    li = rep(lam_im)
    ldt = rep(jnp.broadcast_to(log_dt[:, :, None], lam_re.shape))
    tr = lambda b: jnp.transpose(b, (0, 1, 3, 2)).reshape(depth, rows, SSM_STATE)
    spec = pl.BlockSpec((None, rows, SSM_STATE), lambda l: (l, 0, 0))
    shp = jax.ShapeDtypeStruct((depth, rows, SSM_STATE), F32)
    ar, ai, bbr, bbi = pl.pallas_call(
        _s5_prep_kernel, grid=(depth,), in_specs=[spec] * 5, out_specs=[spec] * 4,
        out_shape=[shp] * 4, name="s5_prep")(lr, li, ldt, tr(b_re), tr(b_im))
    return ar, ai, bbr, bbi


def _in_kernel(x_ref, nrm_ref, win_ref, wb_ref, wc_ref, are_ref, aim_ref, d_ref, wglu_ref, wso_ref,
               cos_ref, sin_ref, s0_ref,
               mixa_ref, q_ref, k_ref, v_ref, gret_ref, gb_ref, st_ref, sbuf, ybuf):
    c = pl.program_id(1)
    T, Bb, D = x_ref.shape
    R = T * Bb
    nbt = Bb // SUBLANES
    P2 = SSM_BLK_STATE

    @pl.when(c == 0)
    def _():
        st_ref[...] = s0_ref[...]

    x = x_ref[...].reshape(R, D)
    h = _rms(x, nrm_ref[...]).astype(BF16)

    u = _dot(h, win_ref[:, C_U:C_U + D_SSM])
    ub = u.astype(BF16)
    for i in range(SSM_NBLK):
        sbuf[...] = _dot(ub[:, i * SSM_BLK_CH:(i + 1) * SSM_BLK_CH], wb_ref[i])
        are = jnp.broadcast_to(are_ref[i], (SUBLANES, P2))
        aim = jnp.broadcast_to(aim_ref[i], (SUBLANES, P2))
        for bt in range(nbt):
            r0 = bt * SUBLANES

            def step(t, carry, r0=r0, are=are, aim=aim):
                sre, sim = carry
                row = pl.multiple_of(t * Bb + r0, SUBLANES)
                bre = sbuf[pl.ds(row, SUBLANES), 0:P2]
                bim = sbuf[pl.ds(row, SUBLANES), P2:2 * P2]
                nre = are * sre - aim * sim + bre
                nim = are * sim + aim * sre + bim
                sbuf[pl.ds(row, SUBLANES), 0:P2] = nre
                sbuf[pl.ds(row, SUBLANES), P2:2 * P2] = nim
                return nre, nim

            sre0 = st_ref[i, r0:r0 + SUBLANES, 0:P2]
            sim0 = st_ref[i, r0:r0 + SUBLANES, P2:2 * P2]
            sre, sim = lax.fori_loop(0, T, step, (sre0, sim0))
            st_ref[i, r0:r0 + SUBLANES, 0:P2] = sre
            st_ref[i, r0:r0 + SUBLANES, P2:2 * P2] = sim
        ybuf[:, i * SSM_BLK_CH:(i + 1) * SSM_BLK_CH] = _dot(sbuf[...].astype(BF16), wc_ref[i])
    y = ybuf[...] + d_ref[...] * u
    ya = jax.nn.gelu(y)
    ya = ya * jax.nn.sigmoid(_dot(ya.astype(BF16), wglu_ref[...]))
    yap = _dot(ya.astype(BF16), wso_ref[...])
    ga = _dot(h, win_ref[:, C_GA:C_GA + D_MODEL])
    mixa_ref[...] = (jax.nn.sigmoid(ga) * yap).reshape(T, Bb, D)
    gb = _dot(h, win_ref[:, C_GB:C_GB + D_MODEL])
    gb_ref[...] = jax.nn.sigmoid(gb).reshape(T, Bb, D)

    gret_ref[...] = _dot(h, win_ref[:, C_GR:C_GR + V_W]).reshape(T, Bb, V_W)
    v_ref[...] = _dot(h, win_ref[:, C_V:C_V + V_W]).reshape(T, Bb, V_W)
    cosf = cos_ref[...]
    sinf = sin_ref[...]
    for (c0, out_ref, scale) in ((C_Q, q_ref, None), (C_K, k_ref, RET_DK ** -0.5)):
        z = _dot(h, win_ref[:, c0:c0 + QK_W])
        for hd in range(RET_HEADS):
            zh = z[:, hd * RET_DK:(hd + 1) * RET_DK]
            zr = pltpu.roll(zh, RET_DK // 2, 1)
            rot = zh.reshape(T, Bb, RET_DK) * cosf + zr.reshape(T, Bb, RET_DK) * sinf
            if scale is not None:
                rot = rot * scale
            out_ref[:, :, hd * RET_DK:(hd + 1) * RET_DK] = rot


def _in_stage(l, x, s0, cosf, sinf, W, T, Bb):
    L, B, D = x.shape
    grid = (B // Bb, L // T)
    R = T * Bb
    act = lambda w: pl.BlockSpec((T, Bb, w), lambda bb, c: (c, bb, 0))
    lw = lambda shape: _const_spec((None,) + shape, lambda bb, c: (l,) + (0,) * len(shape))
    st_spec = pl.BlockSpec((SSM_NBLK, Bb, 2 * SSM_BLK_STATE), lambda bb, c: (0, bb, 0))
    in_specs = [
        act(D),
        lw((1, D)), lw((D, IN_COLS)),
        lw((SSM_NBLK, SSM_BLK_CH, 2 * SSM_BLK_STATE)), lw((SSM_NBLK, 2 * SSM_BLK_STATE, SSM_BLK_CH)),
        lw((SSM_NBLK, 1, SSM_BLK_STATE)), lw((SSM_NBLK, 1, SSM_BLK_STATE)),
        lw((1, D_SSM)), lw((D_SSM, D_SSM)), lw((D_SSM, D)),
        pl.BlockSpec((T, 1, RET_DK), lambda bb, c: (c, 0, 0)),
        pl.BlockSpec((T, 1, RET_DK), lambda bb, c: (c, 0, 0)),
        st_spec,
    ]
    out_specs = [act(D), act(QK_W), act(QK_W), act(V_W), act(V_W), act(D), st_spec]
    out_shape = [jax.ShapeDtypeStruct((L, B, w), F32) for w in (D, QK_W, QK_W, V_W, V_W, D)]
    out_shape.append(jax.ShapeDtypeStruct((SSM_NBLK, B, 2 * SSM_BLK_STATE), F32))
    return pl.pallas_call(
        _in_kernel, grid=grid, in_specs=in_specs, out_specs=out_specs, out_shape=out_shape,
        scratch_shapes=[pltpu.VMEM((R, 2 * SSM_BLK_STATE), F32), pltpu.VMEM((R, D_SSM), F32)],
        compiler_params=pltpu.CompilerParams(dimension_semantics=("arbitrary", "arbitrary"),
                                             vmem_limit_bytes=VMEM_LIMIT),
        name="mixer_in")(x, W["norm_mix"], W["w_in"], W["wb"], W["wc"], W["are"], W["aim"], W["d"],
                         W["w_glu"], W["w_ssm_out"], cosf, sinf, s0)


def _ret_kernel(q_ref, k_ref, v_ref, r0_ref, intra_ref, inner_ref, tail_ref, decay_ref, on_ref, r_ref):
    c = pl.program_id(1)
    Bb = q_ref.shape[0]

    @pl.when(c == 0)
    def _():
        r_ref[...] = r0_ref[...]

    def body(b, carry):
        for hd in range(RET_HEADS):
            qb = q_ref[b, :, hd * RET_DK:(hd + 1) * RET_DK].astype(BF16)
            kf = k_ref[b, :, hd * RET_DK:(hd + 1) * RET_DK]
            vb = v_ref[b, :, hd * RET_DV:(hd + 1) * RET_DV].astype(BF16)
            r = r_ref[b, hd]
            sc = lax.dot_general(qb, kf.astype(BF16), (((1,), (1,)), ((), ())),
                                 preferred_element_type=F32) * intra_ref[hd]
            o = _dot(sc.astype(BF16), vb) + _dot(qb, r.astype(BF16)) * inner_ref[hd]
            kt = (kf * tail_ref[hd]).astype(BF16)
            upd = lax.dot_general(kt, vb, (((0,), (0,)), ((), ())), preferred_element_type=F32)
            r_ref[b, hd] = r * decay_ref[hd] + upd
            on = o * lax.rsqrt(jnp.mean(o * o, axis=-1, keepdims=True) + EPS)
            on_ref[b, :, hd * RET_DV:(hd + 1) * RET_DV] = on
        return carry

    lax.fori_loop(0, Bb, body, 0)


def _ret_tables(Tc):
    hidx = jnp.arange(RET_HEADS, dtype=F32)
    log_g = jnp.log1p(-jnp.exp2(-5.0 - hidx))
    idx = jnp.arange(Tc, dtype=F32)
    rel = idx[:, None] - idx[None, :]
    intra = jnp.where(rel >= 0, jnp.exp(jnp.maximum(rel, 0.0)[None] * log_g[:, None, None]), 0.0)
    inner = jnp.exp((idx[None, :] + 1.0) * log_g[:, None])[:, :, None]
    tail = jnp.exp((Tc - 1.0 - idx[None, :]) * log_g[:, None])[:, :, None]
    decay = jnp.exp(Tc * log_g)
    return intra, inner, tail, decay


def _ret_stage(q, k, v, r0, Tc, Bb):
    B, L, _ = q.shape
    intra, inner, tail, decay = _ret_tables(Tc)
    grid = (B // Bb, L // Tc)
    act = lambda w: pl.BlockSpec((Bb, Tc, w), lambda bb, c: (bb, c, 0))
    r_spec = pl.BlockSpec((Bb, RET_HEADS, RET_DK, RET_DV), lambda bb, c: (bb, 0, 0, 0))
    in_specs = [act(QK_W), act(QK_W), act(V_W), r_spec,
                _const_spec((RET_HEADS, Tc, Tc), lambda bb, c: (0, 0, 0)),
                _const_spec((RET_HEADS, Tc, 1), lambda bb, c: (0, 0, 0)),
                _const_spec((RET_HEADS, Tc, 1), lambda bb, c: (0, 0, 0)),
                pl.BlockSpec(memory_space=pltpu.SMEM)]
    return pl.pallas_call(
        _ret_kernel, grid=grid, in_specs=in_specs, out_specs=[act(V_W), r_spec],
        out_shape=[jax.ShapeDtypeStruct((B, L, V_W), F32),
                   jax.ShapeDtypeStruct((B, RET_HEADS, RET_DK, RET_DV), F32)],
        compiler_params=pltpu.CompilerParams(dimension_semantics=("arbitrary", "arbitrary"),
                                             vmem_limit_bytes=VMEM_LIMIT),
        name="retention")(q, k, v, r0, intra, inner, tail, decay)


def _out_kernel(x_ref, mixa_ref, gb_ref, gret_ref, on_ref, wro_ref, wo_ref, nf_ref, wupv_ref, wupg_ref,
                cw_ref, cb_ref, wd_ref, cbuf_ref, x2_ref, cout_ref):
    c = pl.program_id(1)
    T, Bb, D = x_ref.shape
    R = T * Bb

    @pl.when(c == 0)
    def _():
        cout_ref[...] = cbuf_ref[...]

    g = gret_ref[...].reshape(R, V_W)
    og = (g * jax.nn.sigmoid(g)) * on_ref[...].reshape(R, V_W)
    yb = _dot(og.astype(BF16), wro_ref[...])
    mix = mixa_ref[...].reshape(R, D) + gb_ref[...].reshape(R, D) * yb
    x1 = x_ref[...].reshape(R, D) + _dot(mix.astype(BF16), wo_ref[...])
    h2 = _rms(x1, nf_ref[...]).astype(BF16)
    acc = jnp.zeros((R, D), F32)
    for j in range(FF_NT):
        halves = []
        for part, wref in ((0, wupv_ref), (1, wupg_ref)):
            col = part * D_FF + j * FF_TILE
            up = _dot(h2, wref[j])
            carry = cout_ref[:, :, col:col + FF_TILE].reshape((CONV_W - 1) * Bb, FF_TILE)
            ext = jnp.concatenate([carry, up], axis=0)
            hc = cb_ref[:, col:col + FF_TILE]
            for jj in range(CONV_W):
                hc = hc + cw_ref[jj:jj + 1, col:col + FF_TILE] * ext[jj * Bb:jj * Bb + R]
            cout_ref[:, :, col:col + FF_TILE] = ext[R:R + (CONV_W - 1) * Bb].reshape(CONV_W - 1, Bb, FF_TILE)
            halves.append(hc)
        val, gate = halves
        a = (gate * jax.nn.sigmoid(gate)) * val
        acc = acc + _dot(a.astype(BF16), wd_ref[j])
    x2_ref[...] = (x1 + acc).reshape(T, Bb, D)


def _out_stage(l, x, mixa, gb, gret, on, cbuf, W, T, Bb):
    L, B, D = x.shape
    grid = (B // Bb, L // T)
    act = lambda w: pl.BlockSpec((T, Bb, w), lambda bb, c: (c, bb, 0))
    lw = lambda shape: _const_spec((None,) + shape, lambda bb, c: (l,) + (0,) * len(shape))
    cb_spec = pl.BlockSpec((CONV_W - 1, Bb, 2 * D_FF), lambda bb, c: (0, bb, 0))
    in_specs = [act(D), act(D), act(D), act(V_W), act(V_W),
                lw((V_W, D)), lw((D, D)), lw((1, D)),
                lw((FF_NT, D, FF_TILE)), lw((FF_NT, D, FF_TILE)),
                lw((CONV_W, 2 * D_FF)), lw((1, 2 * D_FF)), lw((FF_NT, FF_TILE, D)),
                cb_spec]
    return pl.pallas_call(
        _out_kernel, grid=grid, in_specs=in_specs, out_specs=[act(D), cb_spec],
        out_shape=[jax.ShapeDtypeStruct((L, B, D), F32),
                   jax.ShapeDtypeStruct((CONV_W - 1, B, 2 * D_FF), F32)],
        compiler_params=pltpu.CompilerParams(dimension_semantics=("arbitrary", "arbitrary"),
                                             vmem_limit_bytes=VMEM_LIMIT),
        name="mixer_out_ffn")(x, mixa, gb, gret, on, W["w_ret_out"], W["w_o"], W["norm_ffn"],
                              W["w_up_val"], W["w_up_gate"], W["conv_w"], W["conv_b"], W["w_down"], cbuf)


def _norm_kernel(x_ref, g_ref, o_ref):
    T, Bb, D = x_ref.shape
    o_ref[...] = _rms(x_ref[...].reshape(T * Bb, D), g_ref[...]).reshape(T, Bb, D)


def _final_norm(x, g, T, Bb):
    L, B, D = x.shape
    act = pl.BlockSpec((T, Bb, D), lambda bb, c: (c, bb, 0))
    return pl.pallas_call(
        _norm_kernel, grid=(B // Bb, L // T), in_specs=[act, _const_spec((1, D), lambda bb, c: (0, 0))],
        out_specs=act, out_shape=jax.ShapeDtypeStruct((L, B, D), F32), name="final_norm")(x, g)


def _prep_weights(norm_mix, w_in, lam_re, lam_im, log_dt, b_re, b_im, c_re, c_im, d, w_glu, w_ssm_out,
                  w_ret_out, w_o, norm_ffn, w_up, conv_w, conv_b, w_down):
    depth = w_in.shape[0]
    ar, ai, bbr, bbi = _s5_prep(lam_re, lam_im, log_dt, b_re, b_im)
    eye = jnp.eye(SSM_BLK_GROUPS, dtype=F32)
    blk5 = (depth, SSM_NBLK, SSM_BLK_GROUPS, SSM_GROUP, SSM_STATE)

    def in_mat(bb):
        m = jnp.einsum('ligcp,gh->ligchp', bb.reshape(blk5), eye)
        return m.reshape(depth, SSM_NBLK, SSM_BLK_CH, SSM_BLK_STATE)

    def out_mat(cc):
        m = jnp.einsum('ligcp,gh->ligphc', cc.reshape(blk5), eye)
        return m.reshape(depth, SSM_NBLK, SSM_BLK_STATE, SSM_BLK_CH)

    lam_rows = lambda a: a.reshape(depth, SSM_GROUPS, SSM_GROUP, SSM_STATE)[:, :, 0].reshape(
        depth, SSM_NBLK, 1, SSM_BLK_STATE)
    up_tiles = lambda w: jnp.transpose(w.reshape(depth, D_MODEL, FF_NT, FF_TILE), (0, 2, 1, 3)).astype(BF16)
    return {
        "norm_mix": norm_mix[:, None, :],
        "w_in": w_in.astype(BF16),
        "wb": jnp.concatenate([in_mat(bbr), in_mat(bbi)], axis=-1).astype(BF16),
        "wc": jnp.concatenate([out_mat(c_re), -out_mat(c_im)], axis=2).astype(BF16),
        "are": lam_rows(ar), "aim": lam_rows(ai),
        "d": d[:, None, :],
        "w_glu": w_glu.astype(BF16), "w_ssm_out": w_ssm_out.astype(BF16),
        "w_ret_out": w_ret_out.astype(BF16), "w_o": w_o.astype(BF16),
        "norm_ffn": norm_ffn[:, None, :],
        "w_up_val": up_tiles(w_up[:, :, :D_FF]), "w_up_gate": up_tiles(w_up[:, :, D_FF:]),
        "conv_w": conv_w, "conv_b": conv_b[:, None, :],
        "w_down": w_down.reshape(depth, FF_NT, FF_TILE, D_MODEL).astype(BF16),
    }


def _rope_tables(pos):
    half = RET_DK // 2
    inv = ROPE_BASE ** (-jnp.arange(half, dtype=F32) / half)
    ang = pos[:, None] * inv[None, :]
    cos = jnp.cos(ang)
    sin = jnp.sin(ang)
    cosf = jnp.concatenate([cos, cos], axis=-1)[:, None, :]
    sinf = jnp.concatenate([-sin, sin], axis=-1)[:, None, :]
    return cosf, sinf


def _ssm_state_in(re, im):
    B = re.shape[0]
    f = lambda a: jnp.transpose(a.reshape(B, SSM_NBLK, SSM_BLK_STATE), (1, 0, 2))
    return jnp.concatenate([f(re), f(im)], axis=-1)


def _ssm_state_out(st):
    B = st.shape[1]
    f = lambda a: jnp.transpose(a, (1, 0, 2)).reshape(B, SSM_GROUPS, SSM_STATE)
    return f(st[..., :SSM_BLK_STATE]), f(st[..., SSM_BLK_STATE:])


def _run_group(x, pos, ssm_re, ssm_im, ret0, conv0, W, norm_final, T, Bb, To, Bo, Tc, Bret):
    xt = jnp.transpose(x, (1, 0, 2))
    cosf, sinf = _rope_tables(pos)
    tm = lambda a: jnp.transpose(a, (1, 0, 2))
    outs = {"re": [], "im": [], "ret": [], "conv": []}
    for l in range(DEPTH):
        s0 = _ssm_state_in(ssm_re[l], ssm_im[l])
        mixa, q, k, v, gret, gb, st = _in_stage(l, xt, s0, cosf, sinf, W, T, Bb)
        on, r = _ret_stage(tm(q), tm(k), tm(v), ret0[l], Tc, Bret)
        xt, cout = _out_stage(l, xt, mixa, gb, gret, tm(on), tm(conv0[l]), W, To, Bo)
        sre, sim = _ssm_state_out(st)
        outs["re"].append(sre)
        outs["im"].append(sim)
        outs["ret"].append(r)
        outs["conv"].append(tm(cout))
    y = tm(_final_norm(xt, norm_final[None, :], T, Bb))
    return y, jnp.stack(outs["re"]), jnp.stack(outs["im"]), jnp.stack(outs["ret"]), jnp.stack(outs["conv"])


def kernel(x_prompt, x_sample, state_ssm_re, state_ssm_im, state_ret, state_conv, norm_mix, w_in, ssm_lam_re, ssm_lam_im, ssm_log_dt, ssm_b_re, ssm_b_im, ssm_c_re, ssm_c_im, ssm_d, w_glu, w_ssm_out, w_ret_out, w_o, norm_ffn, w_up, conv_w, conv_b, w_down, norm_final):
    W = _prep_weights(norm_mix, w_in, ssm_lam_re, ssm_lam_im, ssm_log_dt, ssm_b_re, ssm_b_im, ssm_c_re,
                      ssm_c_im, ssm_d, w_glu, w_ssm_out, w_ret_out, w_o, norm_ffn, w_up, conv_w, conv_b, w_down)
    bp, seq, _ = x_prompt.shape
    bs, dseq, _ = x_sample.shape
    depth = w_in.shape[0]
    zeros = lambda *s: jnp.zeros((depth,) + s, F32)
    yp, rep, imp, retp, convp = _run_group(
        x_prompt, jnp.arange(seq, dtype=F32),
        zeros(bp, SSM_GROUPS, SSM_STATE), zeros(bp, SSM_GROUPS, SSM_STATE),
        zeros(bp, RET_HEADS, RET_DK, RET_DV), zeros(bp, CONV_W - 1, 2 * D_FF),
        W, norm_final, T=min(64, seq), Bb=bp, To=min(32, seq), Bo=bp, Tc=min(128, seq), Bret=bp)
    ys, res, ims, rets, convs = _run_group(
        x_sample, PAST_LEN + jnp.arange(dseq, dtype=F32),
        state_ssm_re, state_ssm_im, state_ret, state_conv,
        W, norm_final, T=dseq, Bb=min(64, bs), To=dseq, Bo=min(32, bs), Tc=dseq, Bret=min(8, bs))
    return (yp, ys, rep, imp, retp, convp, res, ims, rets, convs)
```

```python
import functools
import math

import jax
import jax.numpy as jnp
from jax import lax
from jax.experimental import pallas as pl
from jax.experimental.pallas import tpu as pltpu

D_MODEL = 1024
DEPTH = 4
PAST_LEN = 16384
D_SSM = D_MODEL // 2
SSM_GROUP = 16
SSM_GROUPS = D_SSM // SSM_GROUP
SSM_STATE = 64
RET_HEADS = 4
RET_DK = D_MODEL // (2 * RET_HEADS)
RET_DV = 2 * RET_DK
ROPE_BASE = 10000.0
D_FF = ((8 * D_MODEL // 3 + 127) // 128) * 128
CONV_W = 3
EPS = 1e-6
QK_W = RET_HEADS * RET_DK
V_W = RET_HEADS * RET_DV
IN_COLS = D_SSM + 2 * QK_W + 2 * V_W + 2 * D_MODEL
C_U, C_Q, C_K, C_V, C_GR, C_GA, C_GB = 0, D_SSM, D_SSM + QK_W, D_SSM + 2 * QK_W, \
    D_SSM + 2 * QK_W + V_W, D_SSM + 2 * QK_W + 2 * V_W, D_SSM + 2 * QK_W + 2 * V_W + D_MODEL

SUBLANES = 8
LANES = 128
SSM_BLK_CH = LANES
SSM_NBLK = D_SSM // SSM_BLK_CH
SSM_BLK_GROUPS = SSM_BLK_CH // SSM_GROUP
SSM_BLK_STATE = SSM_BLK_GROUPS * SSM_STATE
FF_TILE = 256
FF_NT = D_FF // FF_TILE
VMEM_LIMIT = 56 * 1024 * 1024

BF16 = jnp.bfloat16
F32 = jnp.float32


def _dot(a, b):
    return jnp.dot(a, b, preferred_element_type=F32)


def _rms(x, g):
    ms = jnp.mean(x * x, axis=-1, keepdims=True)
    return x * lax.rsqrt(ms + EPS) * g


def _const_spec(block_shape, index_map):
    return pl.BlockSpec(block_shape, index_map, pipeline_mode=pl.Buffered(1))


def _s5_prep_kernel(lr_ref, li_ref, ldt_ref, br_ref, bi_ref, ar_ref, ai_ref, bbr_ref, bbi_ref):
    lr = lr_ref[...]
    li = li_ref[...]
    dt = jnp.exp(ldt_ref[...])
    mag = jnp.exp(lr * dt)
    ar = mag * jnp.cos(li * dt)
    ai = mag * jnp.sin(li * dt)
    nr = ar - 1.0
    den = lr * lr + li * li
    fr = (nr * lr + ai * li) / den
    fi = (ai * lr - nr * li) / den
    br = br_ref[...]
    bi = bi_ref[...]
    ar_ref[...] = ar
    ai_ref[...] = ai
    bbr_ref[...] = fr * br - fi * bi
    bbi_ref[...] = fr * bi + fi * br


def _s5_prep(lam_re, lam_im, log_dt, b_re, b_im):
    depth = lam_re.shape[0]
    rows = SSM_GROUPS * SSM_GROUP
    rep = lambda a: jnp.repeat(a, SSM_GROUP, axis=1)
    lr = rep(lam_re)
    li = rep(lam_im)
    ldt = rep(jnp.broadcast_to(log_dt[:, :, None], lam_re.shape))
    tr = lambda b: jnp.transpose(b, (0, 1, 3, 2)).reshape(depth, rows, SSM_STATE)
    spec = pl.BlockSpec((None, rows, SSM_STATE), lambda l: (l, 0, 0))
    shp = jax.ShapeDtypeStruct((depth, rows, SSM_STATE), F32)
    ar, ai, bbr, bbi = pl.pallas_call(
        _s5_prep_kernel, grid=(depth,), in_specs=[spec] * 5, out_specs=[spec] * 4,
        out_shape=[shp] * 4, name="s5_prep")(lr, li, ldt, tr(b_re), tr(b_im))
    return ar, ai, bbr, bbi


def _in_kernel(x_ref, nrm_ref, win_ref, wb_ref, wc_ref, are_ref, aim_ref, d_ref, wglu_ref, wso_ref,
               cos_ref, sin_ref, s0_ref,
               mixa_ref, q_ref, k_ref, v_ref, gret_ref, gb_ref, st_ref, sbuf, ybuf):
    c = pl.program_id(1)
    T, Bb, D = x_ref.shape
    R = T * Bb
    nbt = Bb // SUBLANES
    P2 = SSM_BLK_STATE

    @pl.when(c == 0)
    def _():
        st_ref[...] = s0_ref[...]

    x = x_ref[...].reshape(R, D)
    h = _rms(x, nrm_ref[...]).astype(BF16)

    u = _dot(h, win_ref[:, C_U:C_U + D_SSM])
    ub = u.astype(BF16)
    for i in range(SSM_NBLK):
        sbuf[...] = _dot(ub[:, i * SSM_BLK_CH:(i + 1) * SSM_BLK_CH], wb_ref[i])
        are = jnp.broadcast_to(are_ref[i], (SUBLANES, P2))
        aim = jnp.broadcast_to(aim_ref[i], (SUBLANES, P2))
        for bt in range(nbt):
            r0 = bt * SUBLANES

            def step(t, carry, r0=r0, are=are, aim=aim):
                sre, sim = carry
                row = pl.multiple_of(t * Bb + r0, SUBLANES)
                bre = sbuf[pl.ds(row, SUBLANES), 0:P2]
                bim = sbuf[pl.ds(row, SUBLANES), P2:2 * P2]
                nre = are * sre - aim * sim + bre
                nim = are * sim + aim * sre + bim
                sbuf[pl.ds(row, SUBLANES), 0:P2] = nre
                sbuf[pl.ds(row, SUBLANES), P2:2 * P2] = nim
                return nre, nim

            sre0 = st_ref[i, r0:r0 + SUBLANES, 0:P2]
            sim0 = st_ref[i, r0:r0 + SUBLANES, P2:2 * P2]
            sre, sim = lax.fori_loop(0, T, step, (sre0, sim0))
            st_ref[i, r0:r0 + SUBLANES, 0:P2] = sre
            st_ref[i, r0:r0 + SUBLANES, P2:2 * P2] = sim
        ybuf[:, i * SSM_BLK_CH:(i + 1) * SSM_BLK_CH] = _dot(sbuf[...].astype(BF16), wc_ref[i])
    y = ybuf[...] + d_ref[...] * u
    ya = jax.nn.gelu(y)
    ya = ya * jax.nn.sigmoid(_dot(ya.astype(BF16), wglu_ref[...]))
    yap = _dot(ya.astype(BF16), wso_ref[...])
    ga = _dot(h, win_ref[:, C_GA:C_GA + D_MODEL])
    mixa_ref[...] = (jax.nn.sigmoid(ga) * yap).reshape(T, Bb, D)
    gb = _dot(h, win_ref[:, C_GB:C_GB + D_MODEL])
    gb_ref[...] = jax.nn.sigmoid(gb).reshape(T, Bb, D)

    gret_ref[...] = _dot(h, win_ref[:, C_GR:C_GR + V_W]).reshape(T, Bb, V_W)
    v_ref[...] = _dot(h, win_ref[:, C_V:C_V + V_W]).reshape(T, Bb, V_W)
    cosf = cos_ref[...]
    sinf = sin_ref[...]
    for (c0, out_ref, scale) in ((C_Q, q_ref, None), (C_K, k_ref, RET_DK ** -0.5)):
        z = _dot(h, win_ref[:, c0:c0 + QK_W])
        for hd in range(RET_HEADS):
            zh = z[:, hd * RET_DK:(hd + 1) * RET_DK]
            zr = pltpu.roll(zh, RET_DK // 2, 1)
            rot = zh.reshape(T, Bb, RET_DK) * cosf + zr.reshape(T, Bb, RET_DK) * sinf
            if scale is not None:
                rot = rot * scale
            out_ref[:, :, hd * RET_DK:(hd + 1) * RET_DK] = rot


def _in_stage(l, x, s0, cosf, sinf, W, T, Bb):
    L, B, D = x.shape
    grid = (B // Bb, L // T)
    R = T * Bb
    act = lambda w: pl.BlockSpec((T, Bb, w), lambda bb, c: (c, bb, 0))
    lw = lambda shape: _const_spec((None,) + shape, lambda bb, c: (l,) + (0,) * len(shape))
    st_spec = pl.BlockSpec((SSM_NBLK, Bb, 2 * SSM_BLK_STATE), lambda bb, c: (0, bb, 0))
    in_specs = [
        act(D),
        lw((1, D)), lw((D, IN_COLS)),
        lw((SSM_NBLK, SSM_BLK_CH, 2 * SSM_BLK_STATE)), lw((SSM_NBLK, 2 * SSM_BLK_STATE, SSM_BLK_CH)),
        lw((SSM_NBLK, 1, SSM_BLK_STATE)), lw((SSM_NBLK, 1, SSM_BLK_STATE)),
        lw((1, D_SSM)), lw((D_SSM, D_SSM)), lw((D_SSM, D)),
        pl.BlockSpec((T, 1, RET_DK), lambda bb, c: (c, 0, 0)),
        pl.BlockSpec((T, 1, RET_DK), lambda bb, c: (c, 0, 0)),
        st_spec,
    ]
    out_specs = [act(D), act(QK_W), act(QK_W), act(V_W), act(V_W), act(D), st_spec]
    out_shape = [jax.ShapeDtypeStruct((L, B, w), F32) for w in (D, QK_W, QK_W, V_W, V_W, D)]
    out_shape.append(jax.ShapeDtypeStruct((SSM_NBLK, B, 2 * SSM_BLK_STATE), F32))
    return pl.pallas_call(
        _in_kernel, grid=grid, in_specs=in_specs, out_specs=out_specs, out_shape=out_shape,
        scratch_shapes=[pltpu.VMEM((R, 2 * SSM_BLK_STATE), F32), pltpu.VMEM((R, D_SSM), F32)],
        compiler_params=pltpu.CompilerParams(dimension_semantics=("arbitrary", "arbitrary"),
                                             vmem_limit_bytes=VMEM_LIMIT),
        name="mixer_in")(x, W["norm_mix"], W["w_in"], W["wb"], W["wc"], W["are"], W["aim"], W["d"],
                         W["w_glu"], W["w_ssm_out"], cosf, sinf, s0)


def _ret_kernel(q_ref, k_ref, v_ref, r0_ref, intra_ref, inner_ref, tail_ref, decay_ref, on_ref, r_ref):
    c = pl.program_id(1)
    Bb = q_ref.shape[0]

    @pl.when(c == 0)
    def _():
        r_ref[...] = r0_ref[...]

    def body(b, carry):
        for hd in range(RET_HEADS):
            qb = q_ref[b, :, hd * RET_DK:(hd + 1) * RET_DK].astype(BF16)
            kf = k_ref[b, :, hd * RET_DK:(hd + 1) * RET_DK]
            vb = v_ref[b, :, hd * RET_DV:(hd + 1) * RET_DV].astype(BF16)
            r = r_ref[b, hd]
            sc = lax.dot_general(qb, kf.astype(BF16), (((1,), (1,)), ((), ())),
                                 preferred_element_type=F32) * intra_ref[hd]
            o = _dot(sc.astype(BF16), vb) + _dot(qb, r.astype(BF16)) * inner_ref[hd]
            kt = (kf * tail_ref[hd]).astype(BF16)
            upd = lax.dot_general(kt, vb, (((0,), (0,)), ((), ())), preferred_element_type=F32)
            r_ref[b, hd] = r * decay_ref[hd] + upd
            on = o * lax.rsqrt(jnp.mean(o * o, axis=-1, keepdims=True) + EPS)
            on_ref[b, :, hd * RET_DV:(hd + 1) * RET_DV] = on
        return carry

    lax.fori_loop(0, Bb, body, 0)


def _ret_tables(Tc):
    hidx = jnp.arange(RET_HEADS, dtype=F32)
    log_g = jnp.log1p(-jnp.exp2(-5.0 - hidx))
    idx = jnp.arange(Tc, dtype=F32)
    rel = idx[:, None] - idx[None, :]
    intra = jnp.where(rel >= 0, jnp.exp(jnp.maximum(rel, 0.0)[None] * log_g[:, None, None]), 0.0)
    inner = jnp.exp((idx[None, :] + 1.0) * log_g[:, None])[:, :, None]
    tail = jnp.exp((Tc - 1.0 - idx[None, :]) * log_g[:, None])[:, :, None]
    decay = jnp.exp(Tc * log_g)
    return intra, inner, tail, decay


def _ret_stage(q, k, v, r0, Tc, Bb):
    B, L, _ = q.shape
    intra, inner, tail, decay = _ret_tables(Tc)
    grid = (B // Bb, L // Tc)
    act = lambda w: pl.BlockSpec((Bb, Tc, w), lambda bb, c: (bb, c, 0))
    r_spec = pl.BlockSpec((Bb, RET_HEADS, RET_DK, RET_DV), lambda bb, c: (bb, 0, 0, 0))
    in_specs = [act(QK_W), act(QK_W), act(V_W), r_spec,
                _const_spec((RET_HEADS, Tc, Tc), lambda bb, c: (0, 0, 0)),
                _const_spec((RET_HEADS, Tc, 1), lambda bb, c: (0, 0, 0)),
                _const_spec((RET_HEADS, Tc, 1), lambda bb, c: (0, 0, 0)),
                pl.BlockSpec(memory_space=pltpu.SMEM)]
    return pl.pallas_call(
        _ret_kernel, grid=grid, in_specs=in_specs, out_specs=[act(V_W), r_spec],
        out_shape=[jax.ShapeDtypeStruct((B, L, V_W), F32),
                   jax.ShapeDtypeStruct((B, RET_HEADS, RET_DK, RET_DV), F32)],
        compiler_params=pltpu.CompilerParams(dimension_semantics=("arbitrary", "arbitrary"),
                                             vmem_limit_bytes=VMEM_LIMIT),
        name="retention")(q, k, v, r0, intra, inner, tail, decay)


def _out_kernel(x_ref, mixa_ref, gb_ref, gret_ref, on_ref, wro_ref, wo_ref, nf_ref, wupv_ref, wupg_ref,
                cw_ref, cb_ref, wd_ref, cbuf_ref, x2_ref, cout_ref):
    c = pl.program_id(1)
    T, Bb, D = x_ref.shape
    R = T * Bb

    @pl.when(c == 0)
    def _():
        cout_ref[...] = cbuf_ref[...]

    g = gret_ref[...].reshape(R, V_W)
    og = (g * jax.nn.sigmoid(g)) * on_ref[...].reshape(R, V_W)
    yb = _dot(og.astype(BF16), wro_ref[...])
    mix = mixa_ref[...].reshape(R, D) + gb_ref[...].reshape(R, D) * yb
    x1 = x_ref[...].reshape(R, D) + _dot(mix.astype(BF16), wo_ref[...])
    h2 = _rms(x1, nf_ref[...]).astype(BF16)
    acc = jnp.zeros((R, D), F32)
    for j in range(FF_NT):
        halves = []
        for part, wref in ((0, wupv_ref), (1, wupg_ref)):
            col = part * D_FF + j * FF_TILE
            up = _dot(h2, wref[j])
            carry = cout_ref[:, :, col:col + FF_TILE].reshape((CONV_W - 1) * Bb, FF_TILE)
            ext = jnp.concatenate([carry, up], axis=0)
            hc = cb_ref[:, col:col + FF_TILE]
            for jj in range(CONV_W):
                hc = hc + cw_ref[jj:jj + 1, col:col + FF_TILE] * ext[jj * Bb:jj * Bb + R]
            cout_ref[:, :, col:col + FF_TILE] = ext[R:R + (CONV_W - 1) * Bb].reshape(CONV_W - 1, Bb, FF_TILE)
            halves.append(hc)
        val, gate = halves
        a = (gate * jax.nn.sigmoid(gate)) * val
        acc = acc + _dot(a.astype(BF16), wd_ref[j])
    x2_ref[...] = (x1 + acc).reshape(T, Bb, D)


def _out_stage(l, x, mixa, gb, gret, on, cbuf, W, T, Bb):
    L, B, D = x.shape
    grid = (B // Bb, L // T)
    act = lambda w: pl.BlockSpec((T, Bb, w), lambda bb, c: (c, bb, 0))
    lw = lambda shape: _const_spec((None,) + shape, lambda bb, c: (l,) + (0,) * len(shape))
    cb_spec = pl.BlockSpec((CONV_W - 1, Bb, 2 * D_FF), lambda bb, c: (0, bb, 0))
    in_specs = [act(D), act(D), act(D), act(V_W), act(V_W),
                lw((V_W, D)), lw((D, D)), lw((1, D)),
                lw((FF_NT, D, FF_TILE)), lw((FF_NT, D, FF_TILE)),
                lw((CONV_W, 2 * D_FF)), lw((1, 2 * D_FF)), lw((FF_NT, FF_TILE, D)),
                cb_spec]
    return pl.pallas_call(
        _out_kernel, grid=grid, in_specs=in_specs, out_specs=[act(D), cb_spec],
        out_shape=[jax.ShapeDtypeStruct((L, B, D), F32),
                   jax.ShapeDtypeStruct((CONV_W - 1, B, 2 * D_FF), F32)],
        compiler_params=pltpu.CompilerParams(dimension_semantics=("arbitrary", "arbitrary"),
                                             vmem_limit_bytes=VMEM_LIMIT),
        name="mixer_out_ffn")(x, mixa, gb, gret, on, W["w_ret_out"], W["w_o"], W["norm_ffn"],
                              W["w_up_val"], W["w_up_gate"], W["conv_w"], W["conv_b"], W["w_down"], cbuf)


def _mix_kernel(x_ref, nrm_ref, win_ref, wb_ref, wc_ref, are_ref, aim_ref, d_ref, wglu_ref, wso_ref,
                wro_ref, wo_ref, cos_ref, sin_ref, intra_ref, inner_ref, tail_ref, decay_ref,
                x1_ref, st_ref, r_ref,
                hbuf, sbuf, ybuf, mixbuf, qs, ks, vs, os_):
    c = pl.program_id(0)
    T, Bb, D = x_ref.shape
    R = T * Bb
    P2 = SSM_BLK_STATE

    @pl.when(c == 0)
    def _():
        st_ref[...] = jnp.zeros_like(st_ref)
        r_ref[...] = jnp.zeros_like(r_ref)

    hbuf[...] = _rms(x_ref[...].reshape(R, D), nrm_ref[...]).astype(BF16)

    u = _dot(hbuf[...], win_ref[:, C_U:C_U + D_SSM])
    ub = u.astype(BF16)
    for i in range(SSM_NBLK):
        sbuf[...] = _dot(ub[:, i * SSM_BLK_CH:(i + 1) * SSM_BLK_CH], wb_ref[i])
        are = jnp.broadcast_to(are_ref[i], (SUBLANES, P2))
        aim = jnp.broadcast_to(aim_ref[i], (SUBLANES, P2))

        def step(t, carry, are=are, aim=aim):
            sre, sim = carry
            row = pl.multiple_of(t * Bb, SUBLANES)
            bre = sbuf[pl.ds(row, SUBLANES), 0:P2]
            bim = sbuf[pl.ds(row, SUBLANES), P2:2 * P2]
            nre = are * sre - aim * sim + bre
            nim = are * sim + aim * sre + bim
            sbuf[pl.ds(row, SUBLANES), 0:P2] = nre
            sbuf[pl.ds(row, SUBLANES), P2:2 * P2] = nim
            return nre, nim

        sre, sim = lax.fori_loop(0, T, step, (st_ref[i, :, 0:P2], st_ref[i, :, P2:2 * P2]))
        st_ref[i, :, 0:P2] = sre
        st_ref[i, :, P2:2 * P2] = sim
        ybuf[:, i * SSM_BLK_CH:(i + 1) * SSM_BLK_CH] = _dot(sbuf[...].astype(BF16), wc_ref[i])
    y = ybuf[...] + d_ref[...] * u
    ya = jax.nn.gelu(y)
    ya = ya * jax.nn.sigmoid(_dot(ya.astype(BF16), wglu_ref[...]))
    yap = _dot(ya.astype(BF16), wso_ref[...])
    ga = _dot(hbuf[...], win_ref[:, C_GA:C_GA + D_MODEL])
    mixbuf[...] = jax.nn.sigmoid(ga) * yap

    cosf = cos_ref[...]
    sinf = sin_ref[...]
    for (c0, slab, scale) in ((C_Q, qs, None), (C_K, ks, RET_DK ** -0.5)):
        z = _dot(hbuf[...], win_ref[:, c0:c0 + QK_W])
        for hd in range(RET_HEADS):
            zh = z[:, hd * RET_DK:(hd + 1) * RET_DK]
            zr = pltpu.roll(zh, RET_DK // 2, 1)
            rot = zh.reshape(T, Bb, RET_DK) * cosf + zr.reshape(T, Bb, RET_DK) * sinf
            if scale is not None:
                rot = rot * scale
            slab[hd] = rot.reshape(R, RET_DK)
    v = _dot(hbuf[...], win_ref[:, C_V:C_V + V_W])
    for s in range(V_W // LANES):
        vs[s] = v[:, s * LANES:(s + 1) * LANES]

    def seq(b, carry):
        rows = pl.ds(b, T, stride=Bb)
        for hd in range(RET_HEADS):
            qb = qs[hd, rows, :].astype(BF16)
            kf = ks[hd, rows, :]
            vb = jnp.concatenate([vs[2 * hd, rows, :], vs[2 * hd + 1, rows, :]], axis=1).astype(BF16)
            r = r_ref[b, hd]
            sc = lax.dot_general(qb, kf.astype(BF16), (((1,), (1,)), ((), ())),
                                 preferred_element_type=F32) * intra_ref[hd]
            o = _dot(sc.astype(BF16), vb) + _dot(qb, r.astype(BF16)) * inner_ref[hd]
            kt = (kf * tail_ref[hd]).astype(BF16)
            upd = lax.dot_general(kt, vb, (((0,), (0,)), ((), ())), preferred_element_type=F32)
            r_ref[b, hd] = r * decay_ref[hd] + upd
            on = o * lax.rsqrt(jnp.mean(o * o, axis=-1, keepdims=True) + EPS)
            os_[2 * hd, rows, :] = on[:, :LANES]
            os_[2 * hd + 1, rows, :] = on[:, LANES:]
        return carry

    lax.fori_loop(0, Bb, seq, 0)

    g = _dot(hbuf[...], win_ref[:, C_GR:C_GR + V_W])
    on_all = jnp.concatenate([os_[s] for s in range(V_W // LANES)], axis=1)
    og = (g * jax.nn.sigmoid(g)) * on_all
    yb = _dot(og.astype(BF16), wro_ref[...])
    gb = _dot(hbuf[...], win_ref[:, C_GB:C_GB + D_MODEL])
    mix = mixbuf[...] + jax.nn.sigmoid(gb) * yb
    x1 = x_ref[...].reshape(R, D) + _dot(mix.astype(BF16), wo_ref[...])
    x1_ref[...] = x1.reshape(T, Bb, D)


def _mix_stage(l, x, cosf, sinf, W, T):
    L, B, D = x.shape
    R = T * B
    intra, inner, tail, decay = _ret_tables(T)
    act = pl.BlockSpec((T, B, D), lambda c: (c, 0, 0))
    lw = lambda shape: _const_spec((None,) + shape, lambda c: (l,) + (0,) * len(shape))
    cst = lambda shape: _const_spec(shape, lambda c: (0,) * len(shape))
    rope = pl.BlockSpec((T, 1, RET_DK), lambda c: (c, 0, 0))
    st_spec = pl.BlockSpec((SSM_NBLK, B, 2 * SSM_BLK_STATE), lambda c: (0, 0, 0))
    r_spec = pl.BlockSpec((B, RET_HEADS, RET_DK, RET_DV), lambda c: (0, 0, 0, 0))
    in_specs = [
        act, lw((1, D)), lw((D, IN_COLS)),
        lw((SSM_NBLK, SSM_BLK_CH, 2 * SSM_BLK_STATE)), lw((SSM_NBLK, 2 * SSM_BLK_STATE, SSM_BLK_CH)),
        lw((SSM_NBLK, 1, SSM_BLK_STATE)), lw((SSM_NBLK, 1, SSM_BLK_STATE)),
        lw((1, D_SSM)), lw((D_SSM, D_SSM)), lw((D_SSM, D)), lw((V_W, D)), lw((D, D)),
        rope, rope,
        cst((RET_HEADS, T, T)), cst((RET_HEADS, T, 1)), cst((RET_HEADS, T, 1)),
        pl.BlockSpec(memory_space=pltpu.SMEM),
    ]
    slab = lambda n: pltpu.VMEM((n, R, LANES), F32)
    return pl.pallas_call(
        _mix_kernel, grid=(L // T,), in_specs=in_specs, out_specs=[act, st_spec, r_spec],
        out_shape=[jax.ShapeDtypeStruct((L, B, D), F32),
                   jax.ShapeDtypeStruct((SSM_NBLK, B, 2 * SSM_BLK_STATE), F32),
                   jax.ShapeDtypeStruct((B, RET_HEADS, RET_DK, RET_DV), F32)],
        scratch_shapes=[pltpu.VMEM((R, D), BF16), pltpu.VMEM((R, 2 * SSM_BLK_STATE), F32),
                        pltpu.VMEM((R, D_SSM), F32), pltpu.VMEM((R, D), F32),
                        slab(QK_W // LANES), slab(QK_W // LANES), slab(V_W // LANES), slab(V_W // LANES)],
        compiler_params=pltpu.CompilerParams(dimension_semantics=("arbitrary",),
                                             vmem_limit_bytes=VMEM_LIMIT),
        name="mixer_fused")(x, W["norm_mix"], W["w_in"], W["wb"], W["wc"], W["are"], W["aim"], W["d"],
                            W["w_glu"], W["w_ssm_out"], W["w_ret_out"], W["w_o"], cosf, sinf,
                            intra, inner, tail, decay)


def _ffn_kernel(x_ref, nf_ref, wupv_ref, wupg_ref, cw_ref, cb_ref, wd_ref, x2_ref, cout_ref, hbuf):
    c = pl.program_id(0)
    T, Bb, D = x_ref.shape
    R = T * Bb

    @pl.when(c == 0)
    def _():
        cout_ref[...] = jnp.zeros_like(cout_ref)

    x1 = x_ref[...].reshape(R, D)
    hbuf[...] = _rms(x1, nf_ref[...]).astype(BF16)
    x2_ref[...] = x_ref[...]
    for j in range(FF_NT):
        halves = []
        for part, wref in ((0, wupv_ref), (1, wupg_ref)):
            col = part * D_FF + j * FF_TILE
            up = _dot(hbuf[...], wref[j])
            carry = cout_ref[:, :, col:col + FF_TILE].reshape((CONV_W - 1) * Bb, FF_TILE)
            ext = jnp.concatenate([carry, up], axis=0)
            hc = cb_ref[:, col:col + FF_TILE]
            for jj in range(CONV_W):
                hc = hc + cw_ref[jj:jj + 1, col:col + FF_TILE] * ext[jj * Bb:jj * Bb + R]
            cout_ref[:, :, col:col + FF_TILE] = ext[R:R + (CONV_W - 1) * Bb].reshape(CONV_W - 1, Bb, FF_TILE)
            halves.append(hc)
        val, gate = halves
        a = (gate * jax.nn.sigmoid(gate)) * val
        x2_ref[...] += _dot(a.astype(BF16), wd_ref[j]).reshape(T, Bb, D)


def _ffn_stage(l, x, W, T):
    L, B, D = x.shape
    act = pl.BlockSpec((T, B, D), lambda c: (c, 0, 0))
    lw = lambda shape: _const_spec((None,) + shape, lambda c: (l,) + (0,) * len(shape))
    cb_spec = pl.BlockSpec((CONV_W - 1, B, 2 * D_FF), lambda c: (0, 0, 0))
    in_specs = [act, lw((1, D)), lw((FF_NT, D, FF_TILE)), lw((FF_NT, D, FF_TILE)),
                lw((CONV_W, 2 * D_FF)), lw((1, 2 * D_FF)), lw((FF_NT, FF_TILE, D))]
    return pl.pallas_call(
        _ffn_kernel, grid=(L // T,), in_specs=in_specs, out_specs=[act, cb_spec],
        out_shape=[jax.ShapeDtypeStruct((L, B, D), F32),
                   jax.ShapeDtypeStruct((CONV_W - 1, B, 2 * D_FF), F32)],
        scratch_shapes=[pltpu.VMEM((T * B, D), BF16)],
        compiler_params=pltpu.CompilerParams(dimension_semantics=("arbitrary",),
                                             vmem_limit_bytes=VMEM_LIMIT),
        name="conv_ffn")(x, W["norm_ffn"], W["w_up_val"], W["w_up_gate"], W["conv_w"], W["conv_b"],
                         W["w_down"])


def _norm_kernel(x_ref, g_ref, o_ref):
    T, Bb, D = x_ref.shape
    o_ref[...] = _rms(x_ref[...].reshape(T * Bb, D), g_ref[...]).reshape(T, Bb, D)


def _final_norm(x, g, T, Bb):
    L, B, D = x.shape
    act = pl.BlockSpec((T, Bb, D), lambda bb, c: (c, bb, 0))
    return pl.pallas_call(
        _norm_kernel, grid=(B // Bb, L // T), in_specs=[act, _const_spec((1, D), lambda bb, c: (0, 0))],
        out_specs=act, out_shape=jax.ShapeDtypeStruct((L, B, D), F32), name="final_norm")(x, g)


def _prep_weights(norm_mix, w_in, lam_re, lam_im, log_dt, b_re, b_im, c_re, c_im, d, w_glu, w_ssm_out,
                  w_ret_out, w_o, norm_ffn, w_up, conv_w, conv_b, w_down):
    depth = w_in.shape[0]
    ar, ai, bbr, bbi = _s5_prep(lam_re, lam_im, log_dt, b_re, b_im)
    eye = jnp.eye(SSM_BLK_GROUPS, dtype=F32)
    blk5 = (depth, SSM_NBLK, SSM_BLK_GROUPS, SSM_GROUP, SSM_STATE)

    def in_mat(bb):
        m = jnp.einsum('ligcp,gh->ligchp', bb.reshape(blk5), eye)
        return m.reshape(depth, SSM_NBLK, SSM_BLK_CH, SSM_BLK_STATE)

    def out_mat(cc):
        m = jnp.einsum('ligcp,gh->ligphc', cc.reshape(blk5), eye)
        return m.reshape(depth, SSM_NBLK, SSM_BLK_STATE, SSM_BLK_CH)

    lam_rows = lambda a: a.reshape(depth, SSM_GROUPS, SSM_GROUP, SSM_STATE)[:, :, 0].reshape(
        depth, SSM_NBLK, 1, SSM_BLK_STATE)
    up_tiles = lambda w: jnp.transpose(w.reshape(depth, D_MODEL, FF_NT, FF_TILE), (0, 2, 1, 3)).astype(BF16)
    return {
        "norm_mix": norm_mix[:, None, :],
        "w_in": w_in.astype(BF16),
        "wb": jnp.concatenate([in_mat(bbr), in_mat(bbi)], axis=-1).astype(BF16),
        "wc": jnp.concatenate([out_mat(c_re), -out_mat(c_im)], axis=2).astype(BF16),
        "are": lam_rows(ar), "aim": lam_rows(ai),
        "d": d[:, None, :],
        "w_glu": w_glu.astype(BF16), "w_ssm_out": w_ssm_out.astype(BF16),
        "w_ret_out": w_ret_out.astype(BF16), "w_o": w_o.astype(BF16),
        "norm_ffn": norm_ffn[:, None, :],
        "w_up_val": up_tiles(w_up[:, :, :D_FF]), "w_up_gate": up_tiles(w_up[:, :, D_FF:]),
        "conv_w": conv_w, "conv_b": conv_b[:, None, :],
        "w_down": w_down.reshape(depth, FF_NT, FF_TILE, D_MODEL).astype(BF16),
    }


def _rope_tables(pos):
    half = RET_DK // 2
    inv = ROPE_BASE ** (-jnp.arange(half, dtype=F32) / half)
    ang = pos[:, None] * inv[None, :]
    cos = jnp.cos(ang)
    sin = jnp.sin(ang)
    cosf = jnp.concatenate([cos, cos], axis=-1)[:, None, :]
    sinf = jnp.concatenate([-sin, sin], axis=-1)[:, None, :]
    return cosf, sinf


def _ssm_state_in(re, im):
    B = re.shape[0]
    f = lambda a: jnp.transpose(a.reshape(B, SSM_NBLK, SSM_BLK_STATE), (1, 0, 2))
    return jnp.concatenate([f(re), f(im)], axis=-1)


def _ssm_state_out(st):
    B = st.shape[1]
    f = lambda a: jnp.transpose(a, (1, 0, 2)).reshape(B, SSM_GROUPS, SSM_STATE)
    return f(st[..., :SSM_BLK_STATE]), f(st[..., SSM_BLK_STATE:])


def _run_group(x, pos, ssm_re, ssm_im, ret0, conv0, W, norm_final, T, Bb, To, Bo, Tc, Bret):
    xt = jnp.transpose(x, (1, 0, 2))
    cosf, sinf = _rope_tables(pos)
    tm = lambda a: jnp.transpose(a, (1, 0, 2))
    outs = {"re": [], "im": [], "ret": [], "conv": []}
    for l in range(DEPTH):
        s0 = _ssm_state_in(ssm_re[l], ssm_im[l])
        mixa, q, k, v, gret, gb, st = _in_stage(l, xt, s0, cosf, sinf, W, T, Bb)
        on, r = _ret_stage(tm(q), tm(k), tm(v), ret0[l], Tc, Bret)
        xt, cout = _out_stage(l, xt, mixa, gb, gret, tm(on), tm(conv0[l]), W, To, Bo)
        sre, sim = _ssm_state_out(st)
        outs["re"].append(sre)
        outs["im"].append(sim)
        outs["ret"].append(r)
        outs["conv"].append(tm(cout))
    y = tm(_final_norm(xt, norm_final[None, :], T, Bb))
    return y, jnp.stack(outs["re"]), jnp.stack(outs["im"]), jnp.stack(outs["ret"]), jnp.stack(outs["conv"])


def _run_fresh_group(x, W, norm_final, Tm, Tf):
    B, L, _ = x.shape
    xt = jnp.transpose(x, (1, 0, 2))
    cosf, sinf = _rope_tables(jnp.arange(L, dtype=F32))
    tm = lambda a: jnp.transpose(a, (1, 0, 2))
    outs = {"re": [], "im": [], "ret": [], "conv": []}
    for l in range(DEPTH):
        xt, st, r = _mix_stage(l, xt, cosf, sinf, W, Tm)
        xt, cout = _ffn_stage(l, xt, W, Tf)
        sre, sim = _ssm_state_out(st)
        outs["re"].append(sre)
        outs["im"].append(sim)
        outs["ret"].append(r)
        outs["conv"].append(tm(cout))
    y = tm(_final_norm(xt, norm_final[None, :], Tm, B))
    return y, jnp.stack(outs["re"]), jnp.stack(outs["im"]), jnp.stack(outs["ret"]), jnp.stack(outs["conv"])


def kernel(x_prompt, x_sample, state_ssm_re, state_ssm_im, state_ret, state_conv, norm_mix, w_in, ssm_lam_re, ssm_lam_im, ssm_log_dt, ssm_b_re, ssm_b_im, ssm_c_re, ssm_c_im, ssm_d, w_glu, w_ssm_out, w_ret_out, w_o, norm_ffn, w_up, conv_w, conv_b, w_down, norm_final):
    W = _prep_weights(norm_mix, w_in, ssm_lam_re, ssm_lam_im, ssm_log_dt, ssm_b_re, ssm_b_im, ssm_c_re,
                      ssm_c_im, ssm_d, w_glu, w_ssm_out, w_ret_out, w_o, norm_ffn, w_up, conv_w, conv_b, w_down)
    bp, seq, _ = x_prompt.shape
    bs, dseq, _ = x_sample.shape
    depth = w_in.shape[0]
    yp, rep, imp, retp, convp = _run_fresh_group(x_prompt, W, norm_final, Tm=min(64, seq), Tf=min(64, seq))
    ys, res, ims, rets, convs = _run_group(
        x_sample, PAST_LEN + jnp.arange(dseq, dtype=F32),
        state_ssm_re, state_ssm_im, state_ret, state_conv,
        W, norm_final, T=dseq, Bb=min(64, bs), To=dseq, Bo=min(32, bs), Tc=dseq, Bret=min(8, bs))
    return (yp, ys, rep, imp, retp, convp, res, ims, rets, convs)
```

```python
import functools
import math

import jax
import jax.numpy as jnp
from jax import lax
from jax.experimental import pallas as pl
from jax.experimental.pallas import tpu as pltpu

D_MODEL = 1024
DEPTH = 4
PAST_LEN = 16384
D_SSM = D_MODEL // 2
SSM_GROUP = 16
SSM_GROUPS = D_SSM // SSM_GROUP
SSM_STATE = 64
RET_HEADS = 4
RET_DK = D_MODEL // (2 * RET_HEADS)
RET_DV = 2 * RET_DK
ROPE_BASE = 10000.0
D_FF = ((8 * D_MODEL // 3 + 127) // 128) * 128
CONV_W = 3
EPS = 1e-6
QK_W = RET_HEADS * RET_DK
V_W = RET_HEADS * RET_DV
IN_COLS = D_SSM + 2 * QK_W + 2 * V_W + 2 * D_MODEL
C_U, C_Q, C_K, C_V, C_GR, C_GA, C_GB = 0, D_SSM, D_SSM + QK_W, D_SSM + 2 * QK_W, \
    D_SSM + 2 * QK_W + V_W, D_SSM + 2 * QK_W + 2 * V_W, D_SSM + 2 * QK_W + 2 * V_W + D_MODEL

SUBLANES = 8
LANES = 128
SSM_BLK_CH = LANES
SSM_NBLK = D_SSM // SSM_BLK_CH
SSM_BLK_GROUPS = SSM_BLK_CH // SSM_GROUP
SSM_BLK_STATE = SSM_BLK_GROUPS * SSM_STATE
FF_TILE = 256
FF_NT = D_FF // FF_TILE
VMEM_LIMIT = 56 * 1024 * 1024

BF16 = jnp.bfloat16
F32 = jnp.float32


def _dot(a, b):
    return jnp.dot(a, b, preferred_element_type=F32)


def _rms(x, g):
    ms = jnp.mean(x * x, axis=-1, keepdims=True)
    return x * lax.rsqrt(ms + EPS) * g


def _const_spec(block_shape, index_map):
    return pl.BlockSpec(block_shape, index_map, pipeline_mode=pl.Buffered(1))


def _s5_prep_kernel(lr_ref, li_ref, ldt_ref, br_ref, bi_ref, ar_ref, ai_ref, bbr_ref, bbi_ref):
    lr = lr_ref[...]
    li = li_ref[...]
    dt = jnp.exp(ldt_ref[...])
    mag = jnp.exp(lr * dt)
    ar = mag * jnp.cos(li * dt)
    ai = mag * jnp.sin(li * dt)
    nr = ar - 1.0
    den = lr * lr + li * li
    fr = (nr * lr + ai * li) / den
    fi = (ai * lr - nr * li) / den
    br = br_ref[...]
    bi = bi_ref[...]
    ar_ref[...] = ar
    ai_ref[...] = ai
    bbr_ref[...] = fr * br - fi * bi
    bbi_ref[...] = fr * bi + fi * br


def _s5_prep(lam_re, lam_im, log_dt, b_re, b_im):
    depth = lam_re.shape[0]
    rows = SSM_GROUPS * SSM_GROUP
    rep = lambda a: jnp.repeat(a, SSM_GROUP, axis=1)
    lr = rep(lam_re)
    li = rep(lam_im)
    ldt = rep(jnp.broadcast_to(log_dt[:, :, None], lam_re.shape))
    tr = lambda b: jnp.transpose(b, (0, 1, 3, 2)).reshape(depth, rows, SSM_STATE)
    spec = pl.BlockSpec((None, rows, SSM_STATE), lambda l: (l, 0, 0))
    shp = jax.ShapeDtypeStruct((depth, rows, SSM_STATE), F32)
    ar, ai, bbr, bbi = pl.pallas_call(
        _s5_prep_kernel, grid=(depth,), in_specs=[spec] * 5, out_specs=[spec] * 4,
        out_shape=[shp] * 4, name="s5_prep")(lr, li, ldt, tr(b_re), tr(b_im))
    return ar, ai, bbr, bbi


def _in_kernel(x_ref, nrm_ref, win_ref, wb_ref, wc_ref, are_ref, aim_ref, d_ref, wglu_ref, wso_ref,
               cos_ref, sin_ref, s0_ref,
               mixa_ref, q_ref, k_ref, v_ref, gret_ref, gb_ref, st_ref, sbuf, ybuf):
    c = pl.program_id(1)
    T, Bb, D = x_ref.shape
    R = T * Bb
    nbt = Bb // SUBLANES
    P2 = SSM_BLK_STATE

    @pl.when(c == 0)
    def _():
        st_ref[...] = s0_ref[...]

    x = x_ref[...].reshape(R, D)
    h = _rms(x, nrm_ref[...]).astype(BF16)

    u = _dot(h, win_ref[:, C_U:C_U + D_SSM])
    ub = u.astype(BF16)
    for i in range(SSM_NBLK):
        sbuf[...] = _dot(ub[:, i * SSM_BLK_CH:(i + 1) * SSM_BLK_CH], wb_ref[i])
        are = jnp.broadcast_to(are_ref[i], (SUBLANES, P2))
        aim = jnp.broadcast_to(aim_ref[i], (SUBLANES, P2))
        for bt in range(nbt):
            r0 = bt * SUBLANES

            def step(t, carry, r0=r0, are=are, aim=aim):
                sre, sim = carry
                row = pl.multiple_of(t * Bb + r0, SUBLANES)
                bre = sbuf[pl.ds(row, SUBLANES), 0:P2]
                bim = sbuf[pl.ds(row, SUBLANES), P2:2 * P2]
                nre = are * sre - aim * sim + bre
                nim = are * sim + aim * sre + bim
                sbuf[pl.ds(row, SUBLANES), 0:P2] = nre
                sbuf[pl.ds(row, SUBLANES), P2:2 * P2] = nim
                return nre, nim

            sre0 = st_ref[i, r0:r0 + SUBLANES, 0:P2]
            sim0 = st_ref[i, r0:r0 + SUBLANES, P2:2 * P2]
            sre, sim = lax.fori_loop(0, T, step, (sre0, sim0))
            st_ref[i, r0:r0 + SUBLANES, 0:P2] = sre
            st_ref[i, r0:r0 + SUBLANES, P2:2 * P2] = sim
        ybuf[:, i * SSM_BLK_CH:(i + 1) * SSM_BLK_CH] = _dot(sbuf[...].astype(BF16), wc_ref[i])
    y = ybuf[...] + d_ref[...] * u
    ya = jax.nn.gelu(y)
    ya = ya * jax.nn.sigmoid(_dot(ya.astype(BF16), wglu_ref[...]))
    yap = _dot(ya.astype(BF16), wso_ref[...])
    ga = _dot(h, win_ref[:, C_GA:C_GA + D_MODEL])
    mixa_ref[...] = (jax.nn.sigmoid(ga) * yap).reshape(T, Bb, D)
    gb = _dot(h, win_ref[:, C_GB:C_GB + D_MODEL])
    gb_ref[...] = jax.nn.sigmoid(gb).reshape(T, Bb, D)

    gret_ref[...] = _dot(h, win_ref[:, C_GR:C_GR + V_W]).reshape(T, Bb, V_W)
    v_ref[...] = _dot(h, win_ref[:, C_V:C_V + V_W]).reshape(T, Bb, V_W)
    cosf = cos_ref[...]
    sinf = sin_ref[...]
    for (c0, out_ref, scale) in ((C_Q, q_ref, None), (C_K, k_ref, RET_DK ** -0.5)):
        z = _dot(h, win_ref[:, c0:c0 + QK_W])
        for hd in range(RET_HEADS):
            zh = z[:, hd * RET_DK:(hd + 1) * RET_DK]
            zr = pltpu.roll(zh, RET_DK // 2, 1)
            rot = zh.reshape(T, Bb, RET_DK) * cosf + zr.reshape(T, Bb, RET_DK) * sinf
            if scale is not None:
                rot = rot * scale
            out_ref[:, :, hd * RET_DK:(hd + 1) * RET_DK] = rot


def _in_stage(l, x, s0, cosf, sinf, W, T, Bb):
    L, B, D = x.shape
    grid = (B // Bb, L // T)
    R = T * Bb
    act = lambda w: pl.BlockSpec((T, Bb, w), lambda bb, c: (c, bb, 0))
    lw = lambda shape: _const_spec((None,) + shape, lambda bb, c: (l,) + (0,) * len(shape))
    st_spec = pl.BlockSpec((SSM_NBLK, Bb, 2 * SSM_BLK_STATE), lambda bb, c: (0, bb, 0))
    in_specs = [
        act(D),
        lw((1, D)), lw((D, IN_COLS)),
        lw((SSM_NBLK, SSM_BLK_CH, 2 * SSM_BLK_STATE)), lw((SSM_NBLK, 2 * SSM_BLK_STATE, SSM_BLK_CH)),
        lw((SSM_NBLK, SUBLANES, SSM_BLK_STATE)), lw((SSM_NBLK, SUBLANES, SSM_BLK_STATE)),
        lw((1, D_SSM)), lw((D_SSM, D_SSM)), lw((D_SSM, D)),
        pl.BlockSpec((T, 1, RET_DK), lambda bb, c: (c, 0, 0)),
        pl.BlockSpec((T, 1, RET_DK), lambda bb, c: (c, 0, 0)),
        st_spec,
    ]
    out_specs = [act(D), act(QK_W), act(QK_W), act(V_W), act(V_W), act(D), st_spec]
    out_shape = [jax.ShapeDtypeStruct((L, B, w), F32) for w in (D, QK_W, QK_W, V_W, V_W, D)]
    out_shape.append(jax.ShapeDtypeStruct((SSM_NBLK, B, 2 * SSM_BLK_STATE), F32))
    return pl.pallas_call(
        _in_kernel, grid=grid, in_specs=in_specs, out_specs=out_specs, out_shape=out_shape,
        scratch_shapes=[pltpu.VMEM((R, 2 * SSM_BLK_STATE), F32), pltpu.VMEM((R, D_SSM), F32)],
        compiler_params=pltpu.CompilerParams(dimension_semantics=("arbitrary", "arbitrary"),
                                             vmem_limit_bytes=VMEM_LIMIT),
        name="mixer_in")(x, W["norm_mix"], W["w_in"], W["wb"], W["wc"], W["are"], W["aim"], W["d"],
                         W["w_glu"], W["w_ssm_out"], cosf, sinf, s0)


def _ret_kernel(q_ref, k_ref, v_ref, r0_ref, intra_ref, inner_ref, tail_ref, decay_ref, on_ref, r_ref):
    c = pl.program_id(1)
    Bb = q_ref.shape[0]

    @pl.when(c == 0)
    def _():
        r_ref[...] = r0_ref[...]

    def body(b, carry):
        for hd in range(RET_HEADS):
            qb = q_ref[b, :, hd * RET_DK:(hd + 1) * RET_DK].astype(BF16)
            kf = k_ref[b, :, hd * RET_DK:(hd + 1) * RET_DK]
            vb = v_ref[b, :, hd * RET_DV:(hd + 1) * RET_DV].astype(BF16)
            r = r_ref[b, hd]
            sc = lax.dot_general(qb, kf.astype(BF16), (((1,), (1,)), ((), ())),
                                 preferred_element_type=F32) * intra_ref[hd]
            o = _dot(sc.astype(BF16), vb) + _dot(qb, r.astype(BF16)) * inner_ref[hd]
            kt = (kf * tail_ref[hd]).astype(BF16)
            upd = lax.dot_general(kt, vb, (((0,), (0,)), ((), ())), preferred_element_type=F32)
            r_ref[b, hd] = r * decay_ref[hd] + upd
            on = o * lax.rsqrt(jnp.mean(o * o, axis=-1, keepdims=True) + EPS)
            on_ref[b, :, hd * RET_DV:(hd + 1) * RET_DV] = on
        return carry

    lax.fori_loop(0, Bb, body, 0)


def _ret_tables(Tc):
    hidx = jnp.arange(RET_HEADS, dtype=F32)
    log_g = jnp.log1p(-jnp.exp2(-5.0 - hidx))
    idx = jnp.arange(Tc, dtype=F32)
    rel = idx[:, None] - idx[None, :]
    intra = jnp.where(rel >= 0, jnp.exp(jnp.maximum(rel, 0.0)[None] * log_g[:, None, None]), 0.0)
    inner = jnp.exp((idx[None, :] + 1.0) * log_g[:, None])[:, :, None]
    tail = jnp.exp((Tc - 1.0 - idx[None, :]) * log_g[:, None])[:, :, None]
    decay = jnp.exp(Tc * log_g)
    return intra, inner, tail, decay


def _ret_stage(l, q, k, v, r0, Tc, Bb):
    B, L, _ = q.shape
    intra, inner, tail, decay = _ret_tables(Tc)
    grid = (B // Bb, L // Tc)
    act = lambda w: pl.BlockSpec((Bb, Tc, w), lambda bb, c: (bb, c, 0))
    r_spec = pl.BlockSpec((Bb, RET_HEADS, RET_DK, RET_DV), lambda bb, c: (bb, 0, 0, 0))
    r0_spec = pl.BlockSpec((None, Bb, RET_HEADS, RET_DK, RET_DV), lambda bb, c: (l, bb, 0, 0, 0))
    in_specs = [act(QK_W), act(QK_W), act(V_W), r0_spec,
                _const_spec((RET_HEADS, Tc, Tc), lambda bb, c: (0, 0, 0)),
                _const_spec((RET_HEADS, Tc, 1), lambda bb, c: (0, 0, 0)),
                _const_spec((RET_HEADS, Tc, 1), lambda bb, c: (0, 0, 0)),
                pl.BlockSpec(memory_space=pltpu.SMEM)]
    return pl.pallas_call(
        _ret_kernel, grid=grid, in_specs=in_specs, out_specs=[act(V_W), r_spec],
        out_shape=[jax.ShapeDtypeStruct((B, L, V_W), F32),
                   jax.ShapeDtypeStruct((B, RET_HEADS, RET_DK, RET_DV), F32)],
        compiler_params=pltpu.CompilerParams(dimension_semantics=("arbitrary", "arbitrary"),
                                             vmem_limit_bytes=VMEM_LIMIT),
        name="retention")(q, k, v, r0, intra, inner, tail, decay)


def _out_kernel(x_ref, mixa_ref, gb_ref, gret_ref, on_ref, wro_ref, wo_ref, nf_ref, wup_ref,
                cw_ref, cb_ref, wd_ref, cbuf_ref, x2_ref, cout_ref):
    c = pl.program_id(1)
    T, Bb, D = x_ref.shape
    R = T * Bb

    @pl.when(c == 0)
    def _():
        cout_ref[...] = cbuf_ref[...]

    g = gret_ref[...].reshape(R, V_W)
    og = (g * jax.nn.sigmoid(g)) * on_ref[...].reshape(R, V_W)
    yb = _dot(og.astype(BF16), wro_ref[...])
    mix = mixa_ref[...].reshape(R, D) + gb_ref[...].reshape(R, D) * yb
    x1 = x_ref[...].reshape(R, D) + _dot(mix.astype(BF16), wo_ref[...])
    h2 = _rms(x1, nf_ref[...]).astype(BF16)
    acc = jnp.zeros((R, D), F32)
    for j in range(FF_NT):
        halves = []
        for part in range(2):
            col = part * D_FF + j * FF_TILE
            up = _dot(h2, wup_ref[:, col:col + FF_TILE])
            carry = cout_ref[:, :, col:col + FF_TILE].reshape((CONV_W - 1) * Bb, FF_TILE)
            ext = jnp.concatenate([carry, up], axis=0)
            hc = cb_ref[:, col:col + FF_TILE]
            for jj in range(CONV_W):
                hc = hc + cw_ref[jj:jj + 1, col:col + FF_TILE] * ext[jj * Bb:jj * Bb + R]
            cout_ref[:, :, col:col + FF_TILE] = ext[R:R + (CONV_W - 1) * Bb].reshape(CONV_W - 1, Bb, FF_TILE)
            halves.append(hc)
        val, gate = halves
        a = (gate * jax.nn.sigmoid(gate)) * val
        acc = acc + _dot(a.astype(BF16), wd_ref[j * FF_TILE:(j + 1) * FF_TILE, :])
    x2_ref[...] = (x1 + acc).reshape(T, Bb, D)


def _out_stage(l, x, mixa, gb, gret, on, cbuf, W, T, Bb):
    L, B, D = x.shape
    grid = (B // Bb, L // T)
    act = lambda w: pl.BlockSpec((T, Bb, w), lambda bb, c: (c, bb, 0))
    lw = lambda shape: _const_spec((None,) + shape, lambda bb, c: (l,) + (0,) * len(shape))
    cb_spec = pl.BlockSpec((CONV_W - 1, Bb, 2 * D_FF), lambda bb, c: (0, bb, 0))
    in_specs = [act(D), act(D), act(D), act(V_W), act(V_W),
                lw((V_W, D)), lw((D, D)), lw((1, D)),
                lw((D, 2 * D_FF)),
                lw((CONV_W, 2 * D_FF)), lw((1, 2 * D_FF)), lw((D_FF, D)),
                cb_spec]
    return pl.pallas_call(
        _out_kernel, grid=grid, in_specs=in_specs, out_specs=[act(D), cb_spec],
        out_shape=[jax.ShapeDtypeStruct((L, B, D), F32),
                   jax.ShapeDtypeStruct((CONV_W - 1, B, 2 * D_FF), F32)],
        compiler_params=pltpu.CompilerParams(dimension_semantics=("arbitrary", "arbitrary"),
                                             vmem_limit_bytes=VMEM_LIMIT),
        name="mixer_out_ffn")(x, mixa, gb, gret, on, W["w_ret_out"], W["w_o"], W["norm_ffn"],
                              W["w_up"], W["conv_w"], W["conv_b"], W["w_down"], cbuf)


def _mix_kernel(x_ref, nrm_ref, win_ref, wb_ref, wc_ref, are_ref, aim_ref, d_ref, wglu_ref, wso_ref,
                wro_ref, wo_ref, cos_ref, sin_ref, intra_ref, inner_ref, tail_ref, decay_ref,
                x1_ref, st_ref, r_ref,
                hbuf, sbuf, ybuf, mixbuf, qs, ks, vs, os_, *xs, x_batch_major):
    c = pl.program_id(0)
    if x_batch_major:
        Bb, T, D = x_ref.shape
    else:
        T, Bb, D = x_ref.shape
    R = T * Bb
    P2 = SSM_BLK_STATE
    NS = D // LANES

    @pl.when(c == 0)
    def _():
        st_ref[...] = jnp.zeros_like(st_ref)
        r_ref[...] = jnp.zeros_like(r_ref)

    if x_batch_major:
        xs, = xs
        for b in range(Bb):
            for s in range(NS):
                xs[s, pl.ds(b, T, stride=Bb), :] = x_ref[b, :, s * LANES:(s + 1) * LANES]
        load_x = lambda: jnp.concatenate([xs[s] for s in range(NS)], axis=1)
    else:
        load_x = lambda: x_ref[...].reshape(R, D)

    hbuf[...] = _rms(load_x(), nrm_ref[...]).astype(BF16)

    u = _dot(hbuf[...], win_ref[:, C_U:C_U + D_SSM])
    ub = u.astype(BF16)
    for i0 in range(0, SSM_NBLK, 2):
        blocks = (i0, i0 + 1)
        for j, i in enumerate(blocks):
            sbuf[j] = _dot(ub[:, i * SSM_BLK_CH:(i + 1) * SSM_BLK_CH], wb_ref[i])

        def step(t, carry, blocks=blocks):
            row = pl.multiple_of(t * Bb, SUBLANES)
            new = []
            for j, (sre, sim) in enumerate(carry):
                are = are_ref[blocks[j]]
                aim = aim_ref[blocks[j]]
                bre = sbuf[j, pl.ds(row, SUBLANES), 0:P2]
                bim = sbuf[j, pl.ds(row, SUBLANES), P2:2 * P2]
                nre = are * sre - aim * sim + bre
                nim = are * sim + aim * sre + bim
                sbuf[j, pl.ds(row, SUBLANES), 0:P2] = nre
                sbuf[j, pl.ds(row, SUBLANES), P2:2 * P2] = nim
                new.append((nre, nim))
            return tuple(new)

        fin = lax.fori_loop(0, T, step, tuple((st_ref[i, :, 0:P2], st_ref[i, :, P2:2 * P2]) for i in blocks),
                            unroll=4)
        for j, i in enumerate(blocks):
            st_ref[i, :, 0:P2] = fin[j][0]
            st_ref[i, :, P2:2 * P2] = fin[j][1]
            ybuf[:, i * SSM_BLK_CH:(i + 1) * SSM_BLK_CH] = _dot(sbuf[j].astype(BF16), wc_ref[i])
    y = ybuf[...] + d_ref[...] * u
    ya = jax.nn.gelu(y)
    ya = ya * jax.nn.sigmoid(_dot(ya.astype(BF16), wglu_ref[...]))
    yap = _dot(ya.astype(BF16), wso_ref[...])
    ga = _dot(hbuf[...], win_ref[:, C_GA:C_GA + D_MODEL])
    mixbuf[...] = jax.nn.sigmoid(ga) * yap

    cosf = cos_ref[...]
    sinf = sin_ref[...]
    for (c0, slab, scale) in ((C_Q, qs, None), (C_K, ks, RET_DK ** -0.5)):
        z = _dot(hbuf[...], win_ref[:, c0:c0 + QK_W])
        for hd in range(RET_HEADS):
            zh = z[:, hd * RET_DK:(hd + 1) * RET_DK]
            zr = pltpu.roll(zh, RET_DK // 2, 1)
            rot = zh.reshape(T, Bb, RET_DK) * cosf + zr.reshape(T, Bb, RET_DK) * sinf
            if scale is not None:
                rot = rot * scale
            slab[hd] = rot.reshape(R, RET_DK)
    v = _dot(hbuf[...], win_ref[:, C_V:C_V + V_W])
    for s in range(V_W // LANES):
        vs[s] = v[:, s * LANES:(s + 1) * LANES]

    items = [(b, hd) for b in range(Bb) for hd in range(RET_HEADS)]
    rows = lambda b: pl.ds(b, T, stride=Bb)
    qbs = [qs[hd, rows(b), :].astype(BF16) for b, hd in items]
    kfs = [ks[hd, rows(b), :] for b, hd in items]
    scs = [lax.dot_general(qb, kf.astype(BF16), (((1,), (1,)), ((), ())), preferred_element_type=F32)
           for qb, kf in zip(qbs, kfs)]
    scs = [(sc * intra_ref[hd]).astype(BF16) for sc, (b, hd) in zip(scs, items)]
    vbs = [jnp.concatenate([vs[2 * hd, rows(b), :], vs[2 * hd + 1, rows(b), :]], axis=1).astype(BF16)
           for b, hd in items]
    for (b, hd), qb, sc, vb in zip(items, qbs, scs, vbs):
        o = _dot(sc, vb) + _dot(qb, r_ref[b, hd].astype(BF16)) * inner_ref[hd]
        os_[2 * hd, rows(b), :] = o[:, :LANES]
        os_[2 * hd + 1, rows(b), :] = o[:, LANES:]
    for (b, hd), kf, vb in zip(items, kfs, vbs):
        kt = (kf * tail_ref[hd]).astype(BF16)
        upd = lax.dot_general(kt, vb, (((0,), (0,)), ((), ())), preferred_element_type=F32)
        r_ref[b, hd] = r_ref[b, hd] * decay_ref[hd] + upd

    g = _dot(hbuf[...], win_ref[:, C_GR:C_GR + V_W])
    on_heads = []
    for hd in range(RET_HEADS):
        o = jnp.concatenate([os_[2 * hd], os_[2 * hd + 1]], axis=1)
        on_heads.append(o * lax.rsqrt(jnp.mean(o * o, axis=-1, keepdims=True) + EPS))
    on_all = jnp.concatenate(on_heads, axis=1)
    og = (g * jax.nn.sigmoid(g)) * on_all
    yb = _dot(og.astype(BF16), wro_ref[...])
    gb = _dot(hbuf[...], win_ref[:, C_GB:C_GB + D_MODEL])
    mix = mixbuf[...] + jax.nn.sigmoid(gb) * yb
    x1 = load_x() + _dot(mix.astype(BF16), wo_ref[...])
    x1_ref[...] = x1.reshape(T, Bb, D)


def _mix_stage(l, x, cosf, sinf, W, T, x_batch_major):
    if x_batch_major:
        B, L, D = x.shape
    else:
        L, B, D = x.shape
    R = T * B
    intra, inner, tail, decay = _ret_tables(T)
    inner = jnp.broadcast_to(inner, (RET_HEADS, T, RET_DV))
    tail = jnp.broadcast_to(tail, (RET_HEADS, T, RET_DK))
    act = pl.BlockSpec((T, B, D), lambda c: (c, 0, 0))
    lw = lambda shape: _const_spec((None,) + shape, lambda c: (l,) + (0,) * len(shape))
    cst = lambda shape: _const_spec(shape, lambda c: (0,) * len(shape))
    rope = pl.BlockSpec((T, 1, RET_DK), lambda c: (c, 0, 0))
    st_spec = pl.BlockSpec((SSM_NBLK, B, 2 * SSM_BLK_STATE), lambda c: (0, 0, 0))
    r_spec = pl.BlockSpec((B, RET_HEADS, RET_DK, RET_DV), lambda c: (0, 0, 0, 0))
    x_spec = pl.BlockSpec((B, T, D), lambda c: (0, c, 0)) if x_batch_major else act
    in_specs = [
        x_spec, lw((1, D)), lw((D, IN_COLS)),
        lw((SSM_NBLK, SSM_BLK_CH, 2 * SSM_BLK_STATE)), lw((SSM_NBLK, 2 * SSM_BLK_STATE, SSM_BLK_CH)),
        lw((SSM_NBLK, SUBLANES, SSM_BLK_STATE)), lw((SSM_NBLK, SUBLANES, SSM_BLK_STATE)),
        lw((1, D_SSM)), lw((D_SSM, D_SSM)), lw((D_SSM, D)), lw((V_W, D)), lw((D, D)),
        rope, rope,
        cst((RET_HEADS, T, T)), cst((RET_HEADS, T, RET_DV)), cst((RET_HEADS, T, RET_DK)),
        pl.BlockSpec(memory_space=pltpu.SMEM),
    ]
    slab = lambda n: pltpu.VMEM((n, R, LANES), F32)
    scratch = [pltpu.VMEM((R, D), BF16), pltpu.VMEM((2, R, 2 * SSM_BLK_STATE), F32),
               pltpu.VMEM((R, D_SSM), F32), pltpu.VMEM((R, D), F32),
               slab(QK_W // LANES), slab(QK_W // LANES), slab(V_W // LANES), slab(V_W // LANES)]
    if x_batch_major:
        scratch.append(slab(D // LANES))
    return pl.pallas_call(
        functools.partial(_mix_kernel, x_batch_major=x_batch_major),
        grid=(L // T,), in_specs=in_specs, out_specs=[act, st_spec, r_spec],
        out_shape=[jax.ShapeDtypeStruct((L, B, D), F32),
                   jax.ShapeDtypeStruct((SSM_NBLK, B, 2 * SSM_BLK_STATE), F32),
                   jax.ShapeDtypeStruct((B, RET_HEADS, RET_DK, RET_DV), F32)],
        scratch_shapes=scratch,
        compiler_params=pltpu.CompilerParams(dimension_semantics=("arbitrary",),
                                             vmem_limit_bytes=VMEM_LIMIT),
        name="mixer_fused")(x, W["norm_mix"], W["w_in"], W["wb"], W["wc"], W["are"], W["aim"], W["d"],
                            W["w_glu"], W["w_ssm_out"], W["w_ret_out"], W["w_o"], cosf, sinf,
                            intra, inner, tail, decay)


def _ffn_kernel(x_ref, nf_ref, wup_ref, cw_ref, cb_ref, wd_ref, nfin_ref, x2_ref, cout_ref, hbuf, abuf, *ys,
                final):
    c = pl.program_id(0)
    T, Bb, D = x_ref.shape
    R = T * Bb

    @pl.when(c == 0)
    def _():
        cout_ref[...] = jnp.zeros_like(cout_ref)

    x1 = x_ref[...].reshape(R, D)
    hbuf[...] = _rms(x1, nf_ref[...]).astype(BF16)
    cols = lambda j: [part * D_FF + j * FF_TILE for part in range(2)]
    up_proj = lambda j: [_dot(hbuf[...], wup_ref[:, col:col + FF_TILE]) for col in cols(j)]
    ups = up_proj(0)
    for j in range(FF_NT):
        ups_next = up_proj(j + 1) if j + 1 < FF_NT else None
        halves = []
        for col, up in zip(cols(j), ups):
            carry = cout_ref[:, :, col:col + FF_TILE].reshape((CONV_W - 1) * Bb, FF_TILE)
            ext = jnp.concatenate([carry, up], axis=0)
            hc = cb_ref[:, col:col + FF_TILE]
            for jj in range(CONV_W):
                hc = hc + cw_ref[jj:jj + 1, col:col + FF_TILE] * ext[jj * Bb:jj * Bb + R]
            cout_ref[:, :, col:col + FF_TILE] = ext[R:R + (CONV_W - 1) * Bb].reshape(CONV_W - 1, Bb, FF_TILE)
            halves.append(hc)
        val, gate = halves
        abuf[:, j * FF_TILE:(j + 1) * FF_TILE] = ((gate * jax.nn.sigmoid(gate)) * val).astype(BF16)
        ups = ups_next
    x2 = x1 + _dot(abuf[...], wd_ref[...])
    if final:
        ys, = ys
        y = _rms(x2, nfin_ref[...])
        for s in range(D // LANES):
            ys[s] = y[:, s * LANES:(s + 1) * LANES]
        for b in range(Bb):
            for s in range(D // LANES):
                x2_ref[b, :, s * LANES:(s + 1) * LANES] = ys[s, pl.ds(b, T, stride=Bb), :]
    else:
        x2_ref[...] = x2.reshape(T, Bb, D)


def _ffn_stage(l, x, W, norm_final, T, final):
    L, B, D = x.shape
    R = T * B
    act = pl.BlockSpec((T, B, D), lambda c: (c, 0, 0))
    lw = lambda shape: _const_spec((None,) + shape, lambda c: (l,) + (0,) * len(shape))
    cb_spec = pl.BlockSpec((CONV_W - 1, B, 2 * D_FF), lambda c: (0, 0, 0))
    in_specs = [act, lw((1, D)), lw((D, 2 * D_FF)),
                lw((CONV_W, 2 * D_FF)), lw((1, 2 * D_FF)), lw((D_FF, D)),
                _const_spec((1, D), lambda c: (0, 0))]
    scratch = [pltpu.VMEM((R, D), BF16), pltpu.VMEM((R, D_FF), BF16)]
    if final:
        scratch.append(pltpu.VMEM((D // LANES, R, LANES), F32))
        out_spec, out_shape = pl.BlockSpec((B, T, D), lambda c: (0, c, 0)), (B, L, D)
    else:
        out_spec, out_shape = act, (L, B, D)
    return pl.pallas_call(
        functools.partial(_ffn_kernel, final=final),
        grid=(L // T,), in_specs=in_specs, out_specs=[out_spec, cb_spec],
        out_shape=[jax.ShapeDtypeStruct(out_shape, F32),
                   jax.ShapeDtypeStruct((CONV_W - 1, B, 2 * D_FF), F32)],
        scratch_shapes=scratch,
        compiler_params=pltpu.CompilerParams(dimension_semantics=("arbitrary",),
                                             vmem_limit_bytes=VMEM_LIMIT),
        name="conv_ffn")(x, W["norm_ffn"], W["w_up"], W["conv_w"], W["conv_b"],
                         W["w_down"], norm_final)


def _norm_kernel(x_ref, g_ref, o_ref):
    T, Bb, D = x_ref.shape
    o_ref[...] = _rms(x_ref[...].reshape(T * Bb, D), g_ref[...]).reshape(T, Bb, D)


def _final_norm(x, g, T, Bb):
    L, B, D = x.shape
    act = pl.BlockSpec((T, Bb, D), lambda bb, c: (c, bb, 0))
    return pl.pallas_call(
        _norm_kernel, grid=(B // Bb, L // T), in_specs=[act, _const_spec((1, D), lambda bb, c: (0, 0))],
        out_specs=act, out_shape=jax.ShapeDtypeStruct((L, B, D), F32), name="final_norm")(x, g)


def _prep_weights(norm_mix, w_in, lam_re, lam_im, log_dt, b_re, b_im, c_re, c_im, d, w_glu, w_ssm_out,
                  w_ret_out, w_o, norm_ffn, w_up, conv_w, conv_b, w_down):
    depth = w_in.shape[0]
    ar, ai, bbr, bbi = _s5_prep(lam_re, lam_im, log_dt, b_re, b_im)
    eye = jnp.eye(SSM_BLK_GROUPS, dtype=F32)
    blk5 = (depth, SSM_NBLK, SSM_BLK_GROUPS, SSM_GROUP, SSM_STATE)

    def in_mat(bb):
        m = jnp.einsum('ligcp,gh->ligchp', bb.reshape(blk5), eye)
        return m.reshape(depth, SSM_NBLK, SSM_BLK_CH, SSM_BLK_STATE)

    def out_mat(cc):
        m = jnp.einsum('ligcp,gh->ligphc', cc.reshape(blk5), eye)
        return m.reshape(depth, SSM_NBLK, SSM_BLK_STATE, SSM_BLK_CH)

    lam_rows = lambda a: jnp.broadcast_to(
        a.reshape(depth, SSM_GROUPS, SSM_GROUP, SSM_STATE)[:, :, 0].reshape(depth, SSM_NBLK, 1, SSM_BLK_STATE),
        (depth, SSM_NBLK, SUBLANES, SSM_BLK_STATE))
    return {
        "norm_mix": norm_mix[:, None, :],
        "w_in": w_in.astype(BF16),
        "wb": jnp.concatenate([in_mat(bbr), in_mat(bbi)], axis=-1).astype(BF16),
        "wc": jnp.concatenate([out_mat(c_re), -out_mat(c_im)], axis=2).astype(BF16),
        "are": lam_rows(ar), "aim": lam_rows(ai),
        "d": d[:, None, :],
        "w_glu": w_glu.astype(BF16), "w_ssm_out": w_ssm_out.astype(BF16),
        "w_ret_out": w_ret_out.astype(BF16), "w_o": w_o.astype(BF16),
        "norm_ffn": norm_ffn[:, None, :],
        "w_up": w_up.astype(BF16),
        "conv_w": conv_w, "conv_b": conv_b[:, None, :],
        "w_down": w_down.astype(BF16),
    }


def _rope_tables(pos):
    half = RET_DK // 2
    inv = ROPE_BASE ** (-jnp.arange(half, dtype=F32) / half)
    ang = pos[:, None] * inv[None, :]
    cos = jnp.cos(ang)
    sin = jnp.sin(ang)
    cosf = jnp.concatenate([cos, cos], axis=-1)[:, None, :]
    sinf = jnp.concatenate([-sin, sin], axis=-1)[:, None, :]
    return cosf, sinf


def _ssm_state_in(re, im):
    B = re.shape[0]
    f = lambda a: jnp.transpose(a.reshape(B, SSM_NBLK, SSM_BLK_STATE), (1, 0, 2))
    return jnp.concatenate([f(re), f(im)], axis=-1)


def _ssm_state_out(st):
    B = st.shape[1]
    f = lambda a: jnp.transpose(a, (1, 0, 2)).reshape(B, SSM_GROUPS, SSM_STATE)
    return f(st[..., :SSM_BLK_STATE]), f(st[..., SSM_BLK_STATE:])


def _run_group(x, pos, ssm_re, ssm_im, ret0, conv0, W, norm_final, T, Bb, To, Bo, Tc, Bret):
    xt = jnp.transpose(x, (1, 0, 2))
    cosf, sinf = _rope_tables(pos)
    tm = lambda a: jnp.transpose(a, (1, 0, 2))
    outs = {"re": [], "im": [], "ret": [], "conv": []}
    for l in range(DEPTH):
        s0 = _ssm_state_in(ssm_re[l], ssm_im[l])
        mixa, q, k, v, gret, gb, st = _in_stage(l, xt, s0, cosf, sinf, W, T, Bb)
        on, r = _ret_stage(l, tm(q), tm(k), tm(v), ret0, Tc, Bret)
        xt, cout = _out_stage(l, xt, mixa, gb, gret, tm(on), tm(conv0[l]), W, To, Bo)
        sre, sim = _ssm_state_out(st)
        outs["re"].append(sre)
        outs["im"].append(sim)
        outs["ret"].append(r)
        outs["conv"].append(tm(cout))
    y = tm(_final_norm(xt, norm_final[None, :], T, Bb))
    return y, jnp.stack(outs["re"]), jnp.stack(outs["im"]), jnp.stack(outs["ret"]), jnp.stack(outs["conv"])


def _run_fresh_group(x, W, norm_final, Tm, Tf):
    B, L, _ = x.shape
    cosf, sinf = _rope_tables(jnp.arange(L, dtype=F32))
    tm = lambda a: jnp.transpose(a, (1, 0, 2))
    outs = {"re": [], "im": [], "ret": [], "conv": []}
    xt = x
    for l in range(DEPTH):
        xt, st, r = _mix_stage(l, xt, cosf, sinf, W, Tm, x_batch_major=(l == 0))
        xt, cout = _ffn_stage(l, xt, W, norm_final[None, :], Tf, final=(l == DEPTH - 1))
        sre, sim = _ssm_state_out(st)
        outs["re"].append(sre)
        outs["im"].append(sim)
        outs["ret"].append(r)
        outs["conv"].append(tm(cout))
    return xt, jnp.stack(outs["re"]), jnp.stack(outs["im"]), jnp.stack(outs["ret"]), jnp.stack(outs["conv"])


def kernel(x_prompt, x_sample, state_ssm_re, state_ssm_im, state_ret, state_conv, norm_mix, w_in, ssm_lam_re, ssm_lam_im, ssm_log_dt, ssm_b_re, ssm_b_im, ssm_c_re, ssm_c_im, ssm_d, w_glu, w_ssm_out, w_ret_out, w_o, norm_ffn, w_up, conv_w, conv_b, w_down, norm_final):
    W = _prep_weights(norm_mix, w_in, ssm_lam_re, ssm_lam_im, ssm_log_dt, ssm_b_re, ssm_b_im, ssm_c_re,
                      ssm_c_im, ssm_d, w_glu, w_ssm_out, w_ret_out, w_o, norm_ffn, w_up, conv_w, conv_b, w_down)
    bp, seq, _ = x_prompt.shape
    bs, dseq, _ = x_sample.shape
    depth = w_in.shape[0]
    yp, rep, imp, retp, convp = _run_fresh_group(x_prompt, W, norm_final, Tm=min(64, seq), Tf=min(64, seq))
    ys, res, ims, rets, convs = _run_group(
        x_sample, PAST_LEN + jnp.arange(dseq, dtype=F32),
        state_ssm_re, state_ssm_im, state_ret, state_conv,
        W, norm_final, T=dseq, Bb=min(64, bs), To=dseq, Bo=min(32, bs), Tc=dseq, Bret=min(8, bs))
    return (yp, ys, rep, imp, retp, convp, res, ims, rets, convs)
```

```python
import functools

import jax
import jax.numpy as jnp
from jax import lax
from jax.experimental import pallas as pl
from jax.experimental.pallas import tpu as pltpu

D_MODEL = 1024
DEPTH = 4
PAST_LEN = 16384
D_SSM = D_MODEL // 2
SSM_GROUP = 16
SSM_GROUPS = D_SSM // SSM_GROUP
SSM_STATE = 64
RET_HEADS = 4
RET_DK = D_MODEL // (2 * RET_HEADS)
RET_DV = 2 * RET_DK
ROPE_BASE = 10000.0
D_FF = ((8 * D_MODEL // 3 + 127) // 128) * 128
CONV_W = 3
EPS = 1e-6
QK_W = RET_HEADS * RET_DK
V_W = RET_HEADS * RET_DV
IN_COLS = D_SSM + 2 * QK_W + 2 * V_W + 2 * D_MODEL
C_U = 0
C_Q = C_U + D_SSM
C_K = C_Q + QK_W
C_V = C_K + QK_W
C_GR = C_V + V_W
C_GA = C_GR + V_W
C_GB = C_GA + D_MODEL

SUBLANES = 8
LANES = 128
SSM_BLK_CH = LANES
SSM_NBLK = D_SSM // SSM_BLK_CH
SSM_BLK_GROUPS = SSM_BLK_CH // SSM_GROUP
SSM_BLK_STATE = SSM_BLK_GROUPS * SSM_STATE
FF_TILE = 256
FF_NT = D_FF // FF_TILE
VMEM_LIMIT = 56 * 1024 * 1024
SCAN_UNROLL = 4

BF16 = jnp.bfloat16
F32 = jnp.float32


def _dot(a, b):
    return jnp.dot(a, b, preferred_element_type=F32)


def _rms(x, g):
    ms = jnp.mean(x * x, axis=-1, keepdims=True)
    return x * lax.rsqrt(ms + EPS) * g


def _const_spec(block_shape, index_map):
    return pl.BlockSpec(block_shape, index_map, pipeline_mode=pl.Buffered(1))


def _s5_prep_kernel(lr_ref, li_ref, ldt_ref, br_ref, bi_ref, ar_ref, ai_ref, bbr_ref, bbi_ref):
    lr = lr_ref[...]
    li = li_ref[...]
    dt = jnp.exp(ldt_ref[...])
    mag = jnp.exp(lr * dt)
    ar = mag * jnp.cos(li * dt)
    ai = mag * jnp.sin(li * dt)
    nr = ar - 1.0
    den = lr * lr + li * li
    fr = (nr * lr + ai * li) / den
    fi = (ai * lr - nr * li) / den
    br = br_ref[...]
    bi = bi_ref[...]
    ar_ref[...] = ar
    ai_ref[...] = ai
    bbr_ref[...] = fr * br - fi * bi
    bbi_ref[...] = fr * bi + fi * br


def _s5_prep(lam_re, lam_im, log_dt, b_re, b_im):
    depth = lam_re.shape[0]
    rows = SSM_GROUPS * SSM_GROUP
    rep = lambda a: jnp.repeat(a, SSM_GROUP, axis=1)
    lr = rep(lam_re)
    li = rep(lam_im)
    ldt = rep(jnp.broadcast_to(log_dt[:, :, None], lam_re.shape))
    tr = lambda b: jnp.transpose(b, (0, 1, 3, 2)).reshape(depth, rows, SSM_STATE)
    spec = pl.BlockSpec((None, rows, SSM_STATE), lambda l: (l, 0, 0))
    shp = jax.ShapeDtypeStruct((depth, rows, SSM_STATE), F32)
    ar, ai, bbr, bbi = pl.pallas_call(
        _s5_prep_kernel, grid=(depth,), in_specs=[spec] * 5, out_specs=[spec] * 4,
        out_shape=[shp] * 4, name="s5_prep")(lr, li, ldt, tr(b_re), tr(b_im))
    return ar, ai, bbr, bbi


def _mix_kernel(*refs, x_batch_major, carried, stacked_ret):
    refs = list(refs)
    (x_ref, nrm_ref, win_ref, wb_ref, wc_ref, are_ref, aim_ref, d_ref, wglu_ref, wso_ref, wro_ref, wo_ref,
     cos_ref, sin_ref, intra_ref, inner_ref, tail_ref, decay_ref) = refs[:18]
    refs = refs[18:]
    if carried:
        s0_ref, r0_ref = refs[:2]
        refs = refs[2:]
    if stacked_ret:
        refs = refs[1:]
    x1_ref, st_ref, r_ref, hbuf, sbuf, ybuf, mixbuf, qs, ks, vs, os_ = refs[:11]
    refs = refs[11:]

    c = pl.program_id(1)
    if x_batch_major:
        Bb, T, D = x_ref.shape
    else:
        T, Bb, D = x_ref.shape
    R = T * Bb
    P2 = SSM_BLK_STATE
    NS = D // LANES

    @pl.when(c == 0)
    def _():
        if carried:
            st_ref[...] = s0_ref[...]
            r_ref[...] = r0_ref[...]
        else:
            st_ref[...] = jnp.zeros_like(st_ref)
            r_ref[...] = jnp.zeros_like(r_ref)

    if x_batch_major:
        xs, = refs
        for b in range(Bb):
            for s in range(NS):
                xs[s, pl.ds(b, T, stride=Bb), :] = x_ref[b, :, s * LANES:(s + 1) * LANES]
        load_x = lambda: jnp.concatenate([xs[s] for s in range(NS)], axis=1)
    else:
        load_x = lambda: x_ref[...].reshape(R, D)

    hbuf[...] = _rms(load_x(), nrm_ref[...]).astype(BF16)

    u = _dot(hbuf[...], win_ref[:, C_U:C_U + D_SSM])
    ub = u.astype(BF16)
    for i0 in range(0, SSM_NBLK, 2):
        blocks = (i0, i0 + 1)
        for j, i in enumerate(blocks):
            sbuf[j] = _dot(ub[:, i * SSM_BLK_CH:(i + 1) * SSM_BLK_CH], wb_ref[i])

        def step(t, carry, blocks=blocks):
            row = pl.multiple_of(t * Bb, SUBLANES)
            new = []
            for j, (sre, sim) in enumerate(carry):
                are = are_ref[blocks[j]]
                aim = aim_ref[blocks[j]]
                bre = sbuf[j, pl.ds(row, SUBLANES), 0:P2]
                bim = sbuf[j, pl.ds(row, SUBLANES), P2:2 * P2]
                nre = are * sre - aim * sim + bre
                nim = are * sim + aim * sre + bim
                sbuf[j, pl.ds(row, SUBLANES), 0:P2] = nre
                sbuf[j, pl.ds(row, SUBLANES), P2:2 * P2] = nim
                new.append((nre, nim))
            return tuple(new)

        fin = lax.fori_loop(0, T, step, tuple((st_ref[i, :, 0:P2], st_ref[i, :, P2:2 * P2]) for i in blocks),
                            unroll=SCAN_UNROLL)
        for j, i in enumerate(blocks):
            st_ref[i, :, 0:P2] = fin[j][0]
            st_ref[i, :, P2:2 * P2] = fin[j][1]
            ybuf[:, i * SSM_BLK_CH:(i + 1) * SSM_BLK_CH] = _dot(sbuf[j].astype(BF16), wc_ref[i])
    y = ybuf[...] + d_ref[...] * u
    ya = jax.nn.gelu(y)
    ya = ya * jax.nn.sigmoid(_dot(ya.astype(BF16), wglu_ref[...]))
    yap = _dot(ya.astype(BF16), wso_ref[...])
    ga = _dot(hbuf[...], win_ref[:, C_GA:C_GA + D_MODEL])
    mixbuf[...] = jax.nn.sigmoid(ga) * yap

    cosf = cos_ref[...]
    sinf = sin_ref[...]
    for (c0, slab, scale) in ((C_Q, qs, None), (C_K, ks, RET_DK ** -0.5)):
        z = _dot(hbuf[...], win_ref[:, c0:c0 + QK_W])
        for hd in range(RET_HEADS):
            zh = z[:, hd * RET_DK:(hd + 1) * RET_DK]
            zr = pltpu.roll(zh, RET_DK // 2, 1)
            rot = zh.reshape(T, Bb, RET_DK) * cosf + zr.reshape(T, Bb, RET_DK) * sinf
            if scale is not None:
                rot = rot * scale
            slab[hd] = rot.reshape(R, RET_DK)
    v = _dot(hbuf[...], win_ref[:, C_V:C_V + V_W])
    for s in range(V_W // LANES):
        vs[s] = v[:, s * LANES:(s + 1) * LANES]

    items = [(b, hd) for b in range(Bb) for hd in range(RET_HEADS)]
    rows = lambda b: pl.ds(b, T, stride=Bb)
    qbs = [qs[hd, rows(b), :].astype(BF16) for b, hd in items]
    kfs = [ks[hd, rows(b), :] for b, hd in items]
    scs = [lax.dot_general(qb, kf.astype(BF16), (((1,), (1,)), ((), ())), preferred_element_type=F32)
           for qb, kf in zip(qbs, kfs)]
    scs = [(sc * intra_ref[hd]).astype(BF16) for sc, (b, hd) in zip(scs, items)]
    vbs = [jnp.concatenate([vs[2 * hd, rows(b), :], vs[2 * hd + 1, rows(b), :]], axis=1).astype(BF16)
           for b, hd in items]
    for (b, hd), qb, sc, vb in zip(items, qbs, scs, vbs):
        o = _dot(sc, vb) + _dot(qb, r_ref[b, hd].astype(BF16)) * inner_ref[hd]
        os_[2 * hd, rows(b), :] = o[:, :LANES]
        os_[2 * hd + 1, rows(b), :] = o[:, LANES:]
    for (b, hd), kf, vb in zip(items, kfs, vbs):
        kt = (kf * tail_ref[hd]).astype(BF16)
        upd = lax.dot_general(kt, vb, (((0,), (0,)), ((), ())), preferred_element_type=F32)
        r_ref[b, hd] = r_ref[b, hd] * decay_ref[hd] + upd

    g = _dot(hbuf[...], win_ref[:, C_GR:C_GR + V_W])
    on_heads = []
    for hd in range(RET_HEADS):
        o = jnp.concatenate([os_[2 * hd], os_[2 * hd + 1]], axis=1)
        on_heads.append(o * lax.rsqrt(jnp.mean(o * o, axis=-1, keepdims=True) + EPS))
    on_all = jnp.concatenate(on_heads, axis=1)
    og = (g * jax.nn.sigmoid(g)) * on_all
    yb = _dot(og.astype(BF16), wro_ref[...])
    gb = _dot(hbuf[...], win_ref[:, C_GB:C_GB + D_MODEL])
    mix = mixbuf[...] + jax.nn.sigmoid(gb) * yb
    x1 = load_x() + _dot(mix.astype(BF16), wo_ref[...])
    x1_ref[...] = x1.reshape(T, Bb, D)


def _ret_tables(Tc):
    hidx = jnp.arange(RET_HEADS, dtype=F32)
    log_g = jnp.log1p(-jnp.exp2(-5.0 - hidx))
    idx = jnp.arange(Tc, dtype=F32)
    rel = idx[:, None] - idx[None, :]
    intra = jnp.where(rel >= 0, jnp.exp(jnp.maximum(rel, 0.0)[None] * log_g[:, None, None]), 0.0)
    inner = jnp.exp((idx[None, :] + 1.0) * log_g[:, None])[:, :, None]
    tail = jnp.exp((Tc - 1.0 - idx[None, :]) * log_g[:, None])[:, :, None]
    decay = jnp.exp(Tc * log_g)
    return (intra, jnp.broadcast_to(inner, (RET_HEADS, Tc, RET_DV)),
            jnp.broadcast_to(tail, (RET_HEADS, Tc, RET_DK)), decay)


def _mix_stage(l, x, cosf, sinf, W, T, Bb, x_batch_major, s0=None, ret0=None, ret_acc=None):
    if x_batch_major:
        B, L, D = x.shape
    else:
        L, B, D = x.shape
    carried = s0 is not None
    stacked_ret = carried and ret_acc is not None
    R = T * Bb
    intra, inner, tail, decay = _ret_tables(T)
    act = pl.BlockSpec((T, Bb, D), lambda bb, c: (c, bb, 0))
    lw = lambda shape: _const_spec((None,) + shape, lambda bb, c: (l,) + (0,) * len(shape))
    cst = lambda shape: _const_spec(shape, lambda bb, c: (0,) * len(shape))
    rope = pl.BlockSpec((T, 1, RET_DK), lambda bb, c: (c, 0, 0))
    st_spec = pl.BlockSpec((SSM_NBLK, Bb, 2 * SSM_BLK_STATE), lambda bb, c: (0, bb, 0))
    r_layer = pl.BlockSpec((None, Bb, RET_HEADS, RET_DK, RET_DV), lambda bb, c: (l, bb, 0, 0, 0))
    x_spec = pl.BlockSpec((Bb, T, D), lambda bb, c: (bb, c, 0)) if x_batch_major else act
    in_specs = [
        x_spec, lw((1, D)), lw((D, IN_COLS)),
        lw((SSM_NBLK, SSM_BLK_CH, 2 * SSM_BLK_STATE)), lw((SSM_NBLK, 2 * SSM_BLK_STATE, SSM_BLK_CH)),
        lw((SSM_NBLK, SUBLANES, SSM_BLK_STATE)), lw((SSM_NBLK, SUBLANES, SSM_BLK_STATE)),
        lw((1, D_SSM)), lw((D_SSM, D_SSM)), lw((D_SSM, D)), lw((V_W, D)), lw((D, D)),
        rope, rope,
        cst((RET_HEADS, T, T)), cst((RET_HEADS, T, RET_DV)), cst((RET_HEADS, T, RET_DK)),
        pl.BlockSpec(memory_space=pltpu.SMEM),
    ]
    args = [x, W["norm_mix"], W["w_in"], W["wb"], W["wc"], W["are"], W["aim"], W["d"],
            W["w_glu"], W["w_ssm_out"], W["w_ret_out"], W["w_o"], cosf, sinf, intra, inner, tail, decay]
    aliases = {}
    if carried:
        in_specs += [st_spec, r_layer]
        args += [s0, ret0]
        r_spec = r_layer
        r_shape = jax.ShapeDtypeStruct(ret0.shape, F32)
        if stacked_ret:
            in_specs.append(pl.BlockSpec(memory_space=pl.ANY))
            args.append(ret_acc)
            aliases = {len(args) - 1: 2}
    else:
        r_spec = pl.BlockSpec((Bb, RET_HEADS, RET_DK, RET_DV), lambda bb, c: (bb, 0, 0, 0))
        r_shape = jax.ShapeDtypeStruct((B, RET_HEADS, RET_DK, RET_DV), F32)
    slab = lambda n: pltpu.VMEM((n, R, LANES), F32)
    scratch = [pltpu.VMEM((R, D), BF16), pltpu.VMEM((2, R, 2 * SSM_BLK_STATE), F32),
               pltpu.VMEM((R, D_SSM), F32), pltpu.VMEM((R, D), F32),
               slab(QK_W // LANES), slab(QK_W // LANES), slab(V_W // LANES), slab(V_W // LANES)]
    if x_batch_major:
        scratch.append(slab(D // LANES))
    return pl.pallas_call(
        functools.partial(_mix_kernel, x_batch_major=x_batch_major, carried=carried, stacked_ret=stacked_ret),
        grid=(B // Bb, L // T), in_specs=in_specs, out_specs=[act, st_spec, r_spec],
        out_shape=[jax.ShapeDtypeStruct((L, B, D), F32),
                   jax.ShapeDtypeStruct((SSM_NBLK, B, 2 * SSM_BLK_STATE), F32), r_shape],
        scratch_shapes=scratch, input_output_aliases=aliases,
        compiler_params=pltpu.CompilerParams(dimension_semantics=("arbitrary", "arbitrary"),
                                             vmem_limit_bytes=VMEM_LIMIT),
        name="mixer_fused")(*args)


def _ffn_kernel(*refs, carried, final):
    refs = list(refs)
    x_ref, nf_ref, wup_ref, cw_ref, cb_ref, wd_ref, nfin_ref = refs[:7]
    refs = refs[7:]
    if carried:
        cbuf_ref = refs[0]
        refs = refs[1:]
    x2_ref, cout_ref, hbuf, abuf = refs[:4]
    refs = refs[4:]

    c = pl.program_id(1)
    T, Bb, D = x_ref.shape
    R = T * Bb

    @pl.when(c == 0)
    def _():
        if carried:
            cout_ref[...] = cbuf_ref[...]
        else:
            cout_ref[...] = jnp.zeros_like(cout_ref)

    x1 = x_ref[...].reshape(R, D)
    hbuf[...] = _rms(x1, nf_ref[...]).astype(BF16)
    cols = lambda j: [part * D_FF + j * FF_TILE for part in range(2)]
    up_proj = lambda j: [_dot(hbuf[...], wup_ref[:, col:col + FF_TILE]) for col in cols(j)]
    ups = up_proj(0)
    for j in range(FF_NT):
        ups_next = up_proj(j + 1) if j + 1 < FF_NT else None
        halves = []
        for col, up in zip(cols(j), ups):
            carry = cout_ref[:, :, col:col + FF_TILE].reshape((CONV_W - 1) * Bb, FF_TILE)
            ext = jnp.concatenate([carry, up], axis=0)
            hc = cb_ref[:, col:col + FF_TILE]
            for jj in range(CONV_W):
                hc = hc + cw_ref[jj:jj + 1, col:col + FF_TILE] * ext[jj * Bb:jj * Bb + R]
            cout_ref[:, :, col:col + FF_TILE] = ext[R:R + (CONV_W - 1) * Bb].reshape(CONV_W - 1, Bb, FF_TILE)
            halves.append(hc)
        val, gate = halves
        abuf[:, j * FF_TILE:(j + 1) * FF_TILE] = ((gate * jax.nn.sigmoid(gate)) * val).astype(BF16)
        ups = ups_next
    x2 = x1 + _dot(abuf[...], wd_ref[...])
    if final:
        ys, = refs
        y = _rms(x2, nfin_ref[...])
        for s in range(D // LANES):
            ys[s] = y[:, s * LANES:(s + 1) * LANES]
        for b in range(Bb):
            for s in range(D // LANES):
                x2_ref[b, :, s * LANES:(s + 1) * LANES] = ys[s, pl.ds(b, T, stride=Bb), :]
    else:
        x2_ref[...] = x2.reshape(T, Bb, D)


def _ffn_stage(l, x, W, norm_final, T, Bb, final, cbuf=None):
    L, B, D = x.shape
    R = T * Bb
    carried = cbuf is not None
    act = pl.BlockSpec((T, Bb, D), lambda bb, c: (c, bb, 0))
    lw = lambda shape: _const_spec((None,) + shape, lambda bb, c: (l,) + (0,) * len(shape))
    cb_spec = pl.BlockSpec((CONV_W - 1, Bb, 2 * D_FF), lambda bb, c: (0, bb, 0))
    in_specs = [act, lw((1, D)), lw((D, 2 * D_FF)),
                lw((CONV_W, 2 * D_FF)), lw((1, 2 * D_FF)), lw((D_FF, D)),
                _const_spec((1, D), lambda bb, c: (0, 0))]
    args = [x, W["norm_ffn"], W["w_up"], W["conv_w"], W["conv_b"], W["w_down"], norm_final]
    if carried:
        in_specs.append(cb_spec)
        args.append(cbuf)
    scratch = [pltpu.VMEM((R, D), BF16), pltpu.VMEM((R, D_FF), BF16)]
    if final:
        scratch.append(pltpu.VMEM((D // LANES, R, LANES), F32))
        out_spec, out_shape = pl.BlockSpec((Bb, T, D), lambda bb, c: (bb, c, 0)), (B, L, D)
    else:
        out_spec, out_shape = act, (L, B, D)
    return pl.pallas_call(
        functools.partial(_ffn_kernel, carried=carried, final=final),
        grid=(B // Bb, L // T), in_specs=in_specs, out_specs=[out_spec, cb_spec],
        out_shape=[jax.ShapeDtypeStruct(out_shape, F32),
                   jax.ShapeDtypeStruct((CONV_W - 1, B, 2 * D_FF), F32)],
        scratch_shapes=scratch,
        compiler_params=pltpu.CompilerParams(dimension_semantics=("arbitrary", "arbitrary"),
                                             vmem_limit_bytes=VMEM_LIMIT),
        name="conv_ffn")(*args)


def _prep_weights(norm_mix, w_in, lam_re, lam_im, log_dt, b_re, b_im, c_re, c_im, d, w_glu, w_ssm_out,
                  w_ret_out, w_o, norm_ffn, w_up, conv_w, conv_b, w_down):
    depth = w_in.shape[0]
    ar, ai, bbr, bbi = _s5_prep(lam_re, lam_im, log_dt, b_re, b_im)
    eye = jnp.eye(SSM_BLK_GROUPS, dtype=F32)
    blk5 = (depth, SSM_NBLK, SSM_BLK_GROUPS, SSM_GROUP, SSM_STATE)

    def in_mat(bb):
        m = jnp.einsum('ligcp,gh->ligchp', bb.reshape(blk5), eye)
        return m.reshape(depth, SSM_NBLK, SSM_BLK_CH, SSM_BLK_STATE)

    def out_mat(cc):
        m = jnp.einsum('ligcp,gh->ligphc', cc.reshape(blk5), eye)
        return m.reshape(depth, SSM_NBLK, SSM_BLK_STATE, SSM_BLK_CH)

    lam_rows = lambda a: jnp.broadcast_to(
        a.reshape(depth, SSM_GROUPS, SSM_GROUP, SSM_STATE)[:, :, 0].reshape(depth, SSM_NBLK, 1, SSM_BLK_STATE),
        (depth, SSM_NBLK, SUBLANES, SSM_BLK_STATE))
    return {
        "norm_mix": norm_mix[:, None, :],
        "w_in": w_in.astype(BF16),
        "wb": jnp.concatenate([in_mat(bbr), in_mat(bbi)], axis=-1).astype(BF16),
        "wc": jnp.concatenate([out_mat(c_re), -out_mat(c_im)], axis=2).astype(BF16),
        "are": lam_rows(ar), "aim": lam_rows(ai),
        "d": d[:, None, :],
        "w_glu": w_glu.astype(BF16), "w_ssm_out": w_ssm_out.astype(BF16),
        "w_ret_out": w_ret_out.astype(BF16), "w_o": w_o.astype(BF16),
        "norm_ffn": norm_ffn[:, None, :],
        "w_up": w_up.astype(BF16),
        "conv_w": conv_w, "conv_b": conv_b[:, None, :],
        "w_down": w_down.astype(BF16),
    }


def _rope_tables(pos):
    half = RET_DK // 2
    inv = ROPE_BASE ** (-jnp.arange(half, dtype=F32) / half)
    ang = pos[:, None] * inv[None, :]
    cos = jnp.cos(ang)
    sin = jnp.sin(ang)
    cosf = jnp.concatenate([cos, cos], axis=-1)[:, None, :]
    sinf = jnp.concatenate([-sin, sin], axis=-1)[:, None, :]
    return cosf, sinf


def _ssm_state_in(re, im):
    B = re.shape[0]
    f = lambda a: jnp.transpose(a.reshape(B, SSM_NBLK, SSM_BLK_STATE), (1, 0, 2))
    return jnp.concatenate([f(re), f(im)], axis=-1)


def _ssm_state_out(st):
    B = st.shape[1]
    f = lambda a: jnp.transpose(a, (1, 0, 2)).reshape(B, SSM_GROUPS, SSM_STATE)
    return f(st[..., :SSM_BLK_STATE]), f(st[..., SSM_BLK_STATE:])


def _run_group(x, pos, W, norm_final, mix_tile, ffn_tile, state=None):
    cosf, sinf = _rope_tables(pos)
    tm = lambda a: jnp.transpose(a, (1, 0, 2))
    outs = {"re": [], "im": [], "ret": [], "conv": []}
    xt = x
    ret_acc = None
    for l in range(DEPTH):
        first, last = l == 0, l == DEPTH - 1
        if state is None:
            xt, st, r = _mix_stage(l, xt, cosf, sinf, W, *mix_tile, x_batch_major=first)
            outs["ret"].append(r)
            cbuf = None
        else:
            ssm_re, ssm_im, ret0, conv0 = state
            xt, st, ret_acc = _mix_stage(l, xt, cosf, sinf, W, *mix_tile, x_batch_major=first,
                                         s0=_ssm_state_in(ssm_re[l], ssm_im[l]), ret0=ret0, ret_acc=ret_acc)
            cbuf = tm(conv0[l])
        xt, cout = _ffn_stage(l, xt, W, norm_final[None, :], *ffn_tile, final=last, cbuf=cbuf)
        sre, sim = _ssm_state_out(st)
        outs["re"].append(sre)
        outs["im"].append(sim)
        outs["conv"].append(tm(cout))
    ret = jnp.stack(outs["ret"]) if state is None else ret_acc
    return xt, jnp.stack(outs["re"]), jnp.stack(outs["im"]), ret, jnp.stack(outs["conv"])


def kernel(x_prompt, x_sample, state_ssm_re, state_ssm_im, state_ret, state_conv, norm_mix, w_in, ssm_lam_re, ssm_lam_im, ssm_log_dt, ssm_b_re, ssm_b_im, ssm_c_re, ssm_c_im, ssm_d, w_glu, w_ssm_out, w_ret_out, w_o, norm_ffn, w_up, conv_w, conv_b, w_down, norm_final):
    W = _prep_weights(norm_mix, w_in, ssm_lam_re, ssm_lam_im, ssm_log_dt, ssm_b_re, ssm_b_im, ssm_c_re,
                      ssm_c_im, ssm_d, w_glu, w_ssm_out, w_ret_out, w_o, norm_ffn, w_up, conv_w, conv_b, w_down)
    bp, seq, _ = x_prompt.shape
    bs, dseq, _ = x_sample.shape
    tp = min(64, seq)
    yp, rep, imp, retp, convp = _run_group(
        x_prompt, jnp.arange(seq, dtype=F32), W, norm_final, mix_tile=(tp, bp), ffn_tile=(tp, bp))
    ys, res, ims, rets, convs = _run_group(
        x_sample, PAST_LEN + jnp.arange(dseq, dtype=F32), W, norm_final,
        mix_tile=(dseq, min(8, bs)), ffn_tile=(dseq, min(64, bs)),
        state=(state_ssm_re, state_ssm_im, state_ret, state_conv))
    return (yp, ys, rep, imp, retp, convp, res, ims, rets, convs)
```

```python
import functools

import jax
import jax.numpy as jnp
from jax import lax
from jax.experimental import pallas as pl
from jax.experimental.pallas import tpu as pltpu

D_MODEL = 1024
DEPTH = 4
PAST_LEN = 16384
D_SSM = D_MODEL // 2
SSM_GROUP = 16
SSM_GROUPS = D_SSM // SSM_GROUP
SSM_STATE = 64
RET_HEADS = 4
RET_DK = D_MODEL // (2 * RET_HEADS)
RET_DV = 2 * RET_DK
ROPE_BASE = 10000.0
D_FF = ((8 * D_MODEL // 3 + 127) // 128) * 128
CONV_W = 3
EPS = 1e-6
QK_W = RET_HEADS * RET_DK
V_W = RET_HEADS * RET_DV
IN_COLS = D_SSM + 2 * QK_W + 2 * V_W + 2 * D_MODEL
C_U = 0
C_Q = C_U + D_SSM
C_K = C_Q + QK_W
C_V = C_K + QK_W
C_GR = C_V + V_W
C_GA = C_GR + V_W
C_GB = C_GA + D_MODEL

SUBLANES = 8
LANES = 128
SSM_BLK_CH = LANES
SSM_NBLK = D_SSM // SSM_BLK_CH
SSM_BLK_GROUPS = SSM_BLK_CH // SSM_GROUP
SSM_BLK_STATE = SSM_BLK_GROUPS * SSM_STATE
FF_TILE = 256
FF_NT = D_FF // FF_TILE
VMEM_LIMIT = 56 * 1024 * 1024
SCAN_UNROLL = 4

BF16 = jnp.bfloat16
F32 = jnp.float32


def _dot(a, b):
    return jnp.dot(a, b, preferred_element_type=F32)


def _rms(x, g):
    ms = jnp.mean(x * x, axis=-1, keepdims=True)
    return x * lax.rsqrt(ms + EPS) * g


def _const_spec(block_shape, index_map):
    return pl.BlockSpec(block_shape, index_map, pipeline_mode=pl.Buffered(1))


def _s5_prep_kernel(lr_ref, li_ref, ldt_ref, br_ref, bi_ref, ar_ref, ai_ref, bbr_ref, bbi_ref):
    lr = lr_ref[...]
    li = li_ref[...]
    dt = jnp.exp(ldt_ref[...])
    mag = jnp.exp(lr * dt)
    ar = mag * jnp.cos(li * dt)
    ai = mag * jnp.sin(li * dt)
    nr = ar - 1.0
    den = lr * lr + li * li
    fr = (nr * lr + ai * li) / den
    fi = (ai * lr - nr * li) / den
    br = br_ref[...]
    bi = bi_ref[...]
    ar_ref[...] = ar
    ai_ref[...] = ai
    bbr_ref[...] = fr * br - fi * bi
    bbi_ref[...] = fr * bi + fi * br


def _s5_prep(lam_re, lam_im, log_dt, b_re, b_im):
    depth = lam_re.shape[0]
    rows = SSM_GROUPS * SSM_GROUP
    rep = lambda a: jnp.repeat(a, SSM_GROUP, axis=1)
    lr = rep(lam_re)
    li = rep(lam_im)
    ldt = rep(jnp.broadcast_to(log_dt[:, :, None], lam_re.shape))
    tr = lambda b: jnp.transpose(b, (0, 1, 3, 2)).reshape(depth, rows, SSM_STATE)
    spec = pl.BlockSpec((None, rows, SSM_STATE), lambda l: (l, 0, 0))
    shp = jax.ShapeDtypeStruct((depth, rows, SSM_STATE), F32)
    ar, ai, bbr, bbi = pl.pallas_call(
        _s5_prep_kernel, grid=(depth,), in_specs=[spec] * 5, out_specs=[spec] * 4,
        out_shape=[shp] * 4, name="s5_prep")(lr, li, ldt, tr(b_re), tr(b_im))
    return ar, ai, bbr, bbi


def _mix_kernel(*refs, x_batch_major, carried, n_threaded):
    refs = list(refs)
    (x_ref, nrm_ref, win_ref, wb_ref, wc_ref, are_ref, aim_ref, d_ref, wglu_ref, wso_ref, wro_ref, wo_ref,
     cos_ref, sin_ref, intra_ref, inner_ref, tail_ref, decay_ref) = refs[:18]
    refs = refs[18:]
    if carried:
        s0_ref, r0_ref = refs[:2]
        refs = refs[2:]
    refs = refs[n_threaded:]
    x1_ref, st_ref, r_ref, hbuf, sbuf, ybuf, mixbuf, qs, ks, vs, os_ = refs[:11]
    refs = refs[11:]

    c = pl.program_id(1)
    if x_batch_major:
        Bb, T, D = x_ref.shape
    else:
        T, Bb, D = x_ref.shape
    R = T * Bb
    P2 = SSM_BLK_STATE
    NS = D // LANES

    @pl.when(c == 0)
    def _():
        if carried:
            st_ref[...] = s0_ref[...]
            r_ref[...] = r0_ref[...]
        else:
            st_ref[...] = jnp.zeros_like(st_ref)
            r_ref[...] = jnp.zeros_like(r_ref)

    if x_batch_major:
        xs, = refs
        for b in range(Bb):
            for s in range(NS):
                xs[s, pl.ds(b, T, stride=Bb), :] = x_ref[b, :, s * LANES:(s + 1) * LANES]
        load_x = lambda: jnp.concatenate([xs[s] for s in range(NS)], axis=1)
    else:
        load_x = lambda: x_ref[...].reshape(R, D)

    hbuf[...] = _rms(load_x(), nrm_ref[...]).astype(BF16)

    def project_qk():
        cosf = cos_ref[...]
        sinf = sin_ref[...]
        for (c0, slab, scale) in ((C_Q, qs, None), (C_K, ks, RET_DK ** -0.5)):
            z = _dot(hbuf[...], win_ref[:, c0:c0 + QK_W])
            for hd in range(RET_HEADS):
                zh = z[:, hd * RET_DK:(hd + 1) * RET_DK]
                zr = pltpu.roll(zh, RET_DK // 2, 1)
                rot = zh.reshape(T, Bb, RET_DK) * cosf + zr.reshape(T, Bb, RET_DK) * sinf
                if scale is not None:
                    rot = rot * scale
                slab[hd] = rot.reshape(R, RET_DK)

    def project_v():
        v = _dot(hbuf[...], win_ref[:, C_V:C_V + V_W])
        for s in range(V_W // LANES):
            vs[s] = v[:, s * LANES:(s + 1) * LANES]

    u = _dot(hbuf[...], win_ref[:, C_U:C_U + D_SSM])
    ub = u.astype(BF16)
    for i0, side_work in zip(range(0, SSM_NBLK, 2), (project_qk, project_v)):
        blocks = (i0, i0 + 1)
        for j, i in enumerate(blocks):
            sbuf[j] = _dot(ub[:, i * SSM_BLK_CH:(i + 1) * SSM_BLK_CH], wb_ref[i])
        side_work()
        chains = [(j, i, slice(b0, b0 + SUBLANES)) for j, i in enumerate(blocks) for b0 in range(0, Bb, SUBLANES)]
        state = [(st_ref[i, bt, 0:P2], st_ref[i, bt, P2:2 * P2]) for j, i, bt in chains]
        for t in range(T):
            for n, (j, i, bt) in enumerate(chains):
                rows_t = slice(t * Bb + bt.start, t * Bb + bt.stop)
                sre, sim = state[n]
                are = are_ref[i]
                aim = aim_ref[i]
                nre = are * sre - aim * sim + sbuf[j, rows_t, 0:P2]
                nim = are * sim + aim * sre + sbuf[j, rows_t, P2:2 * P2]
                sbuf[j, rows_t, 0:P2] = nre
                sbuf[j, rows_t, P2:2 * P2] = nim
                state[n] = (nre, nim)
        for n, (j, i, bt) in enumerate(chains):
            st_ref[i, bt, 0:P2] = state[n][0]
            st_ref[i, bt, P2:2 * P2] = state[n][1]
        for j, i in enumerate(blocks):
            ybuf[:, i * SSM_BLK_CH:(i + 1) * SSM_BLK_CH] = _dot(sbuf[j].astype(BF16), wc_ref[i])
    y = ybuf[...] + d_ref[...] * u
    ya = jax.nn.gelu(y)
    ya = ya * jax.nn.sigmoid(_dot(ya.astype(BF16), wglu_ref[...]))
    yap = _dot(ya.astype(BF16), wso_ref[...])
    ga = _dot(hbuf[...], win_ref[:, C_GA:C_GA + D_MODEL])
    mixbuf[...] = jax.nn.sigmoid(ga) * yap

    items = [(b, hd) for b in range(Bb) for hd in range(RET_HEADS)]
    rows = lambda b: pl.ds(b, T, stride=Bb)
    qbs = [qs[hd, rows(b), :].astype(BF16) for b, hd in items]
    kfs = [ks[hd, rows(b), :] for b, hd in items]
    scs = [lax.dot_general(qb, kf.astype(BF16), (((1,), (1,)), ((), ())), preferred_element_type=F32)
           for qb, kf in zip(qbs, kfs)]
    scs = [(sc * intra_ref[hd]).astype(BF16) for sc, (b, hd) in zip(scs, items)]
    vbs = [jnp.concatenate([vs[2 * hd, rows(b), :], vs[2 * hd + 1, rows(b), :]], axis=1).astype(BF16)
           for b, hd in items]
    for (b, hd), qb, sc, vb in zip(items, qbs, scs, vbs):
        o = _dot(sc, vb) + _dot(qb, r_ref[b, hd].astype(BF16)) * inner_ref[hd]
        os_[2 * hd, rows(b), :] = o[:, :LANES]
        os_[2 * hd + 1, rows(b), :] = o[:, LANES:]
    for (b, hd), kf, vb in zip(items, kfs, vbs):
        kt = (kf * tail_ref[hd]).astype(BF16)
        upd = lax.dot_general(kt, vb, (((0,), (0,)), ((), ())), preferred_element_type=F32)
        r_ref[b, hd] = r_ref[b, hd] * decay_ref[hd] + upd

    g = _dot(hbuf[...], win_ref[:, C_GR:C_GR + V_W])
    on_heads = []
    for hd in range(RET_HEADS):
        o = jnp.concatenate([os_[2 * hd], os_[2 * hd + 1]], axis=1)
        on_heads.append(o * lax.rsqrt(jnp.mean(o * o, axis=-1, keepdims=True) + EPS))
    on_all = jnp.concatenate(on_heads, axis=1)
    og = (g * jax.nn.sigmoid(g)) * on_all
    yb = _dot(og.astype(BF16), wro_ref[...])
    gb = _dot(hbuf[...], win_ref[:, C_GB:C_GB + D_MODEL])
    mix = mixbuf[...] + jax.nn.sigmoid(gb) * yb
    x1 = load_x() + _dot(mix.astype(BF16), wo_ref[...])
    x1_ref[...] = x1.reshape(T, Bb, D)


def _ret_tables(Tc):
    hidx = jnp.arange(RET_HEADS, dtype=F32)
    log_g = jnp.log1p(-jnp.exp2(-5.0 - hidx))
    idx = jnp.arange(Tc, dtype=F32)
    rel = idx[:, None] - idx[None, :]
    intra = jnp.where(rel >= 0, jnp.exp(jnp.maximum(rel, 0.0)[None] * log_g[:, None, None]), 0.0)
    inner = jnp.exp((idx[None, :] + 1.0) * log_g[:, None])[:, :, None]
    tail = jnp.exp((Tc - 1.0 - idx[None, :]) * log_g[:, None])[:, :, None]
    decay = jnp.exp(Tc * log_g)
    return (intra, jnp.broadcast_to(inner, (RET_HEADS, Tc, RET_DV)),
            jnp.broadcast_to(tail, (RET_HEADS, Tc, RET_DK)), decay)


def _mix_stage(l, x, cosf, sinf, W, T, Bb, x_batch_major, carried_state=None, threaded=()):
    if x_batch_major:
        B, L, D = x.shape
    else:
        L, B, D = x.shape
    carried = carried_state is not None
    R = T * Bb
    intra, inner, tail, decay = _ret_tables(T)
    act = pl.BlockSpec((T, Bb, D), lambda bb, c: (c, bb, 0))
    lw = lambda shape: _const_spec((None,) + shape, lambda bb, c: (l,) + (0,) * len(shape))
    cst = lambda shape: _const_spec(shape, lambda bb, c: (0,) * len(shape))
    rope = pl.BlockSpec((T, 1, RET_DK), lambda bb, c: (c, 0, 0))
    st_spec = pl.BlockSpec((None, SSM_NBLK, Bb, 2 * SSM_BLK_STATE), lambda bb, c: (l, 0, bb, 0))
    r_spec = pl.BlockSpec((None, Bb, RET_HEADS, RET_DK, RET_DV), lambda bb, c: (l, bb, 0, 0, 0))
    x_spec = pl.BlockSpec((Bb, T, D), lambda bb, c: (bb, c, 0)) if x_batch_major else act
    in_specs = [
        x_spec, lw((1, D)), lw((D, IN_COLS)),
        lw((SSM_NBLK, SSM_BLK_CH, 2 * SSM_BLK_STATE)), lw((SSM_NBLK, 2 * SSM_BLK_STATE, SSM_BLK_CH)),
        lw((SSM_NBLK, SUBLANES, SSM_BLK_STATE)), lw((SSM_NBLK, SUBLANES, SSM_BLK_STATE)),
        lw((1, D_SSM)), lw((D_SSM, D_SSM)), lw((D_SSM, D)), lw((V_W, D)), lw((D, D)),
        rope, rope,
        cst((RET_HEADS, T, T)), cst((RET_HEADS, T, RET_DV)), cst((RET_HEADS, T, RET_DK)),
        pl.BlockSpec(memory_space=pltpu.SMEM),
    ]
    args = [x, W["norm_mix"], W["w_in"], W["wb"], W["wc"], W["are"], W["aim"], W["d"],
            W["w_glu"], W["w_ssm_out"], W["w_ret_out"], W["w_o"], cosf, sinf, intra, inner, tail, decay]
    if carried:
        in_specs += [st_spec, r_spec]
        args += list(carried_state)
    aliases = {}
    for k, prev in enumerate(threaded):
        in_specs.append(pl.BlockSpec(memory_space=pl.ANY))
        aliases[len(args)] = 1 + k
        args.append(prev)
    slab = lambda n: pltpu.VMEM((n, R, LANES), F32)
    scratch = [pltpu.VMEM((R, D), BF16), pltpu.VMEM((2, R, 2 * SSM_BLK_STATE), F32),
               pltpu.VMEM((R, D_SSM), F32), pltpu.VMEM((R, D), F32),
               slab(QK_W // LANES), slab(QK_W // LANES), slab(V_W // LANES), slab(V_W // LANES)]
    if x_batch_major:
        scratch.append(slab(D // LANES))
    return pl.pallas_call(
        functools.partial(_mix_kernel, x_batch_major=x_batch_major, carried=carried, n_threaded=len(threaded)),
        grid=(B // Bb, L // T), in_specs=in_specs, out_specs=[act, st_spec, r_spec],
        out_shape=[jax.ShapeDtypeStruct((L, B, D), F32),
                   jax.ShapeDtypeStruct((DEPTH, SSM_NBLK, B, 2 * SSM_BLK_STATE), F32),
                   jax.ShapeDtypeStruct((DEPTH, B, RET_HEADS, RET_DK, RET_DV), F32)],
        scratch_shapes=scratch, input_output_aliases=aliases,
        compiler_params=pltpu.CompilerParams(dimension_semantics=("arbitrary", "arbitrary"),
                                             vmem_limit_bytes=VMEM_LIMIT),
        name="mixer_fused")(*args)


def _ffn_kernel(*refs, carried, final, n_threaded):
    refs = list(refs)
    x_ref, nf_ref, wup_ref, cw_ref, cb_ref, wd_ref, nfin_ref = refs[:7]
    refs = refs[7:]
    if carried:
        cbuf_ref = refs[0]
        refs = refs[1:]
    refs = refs[n_threaded:]
    x2_ref, cout_ref, hbuf, abuf = refs[:4]
    refs = refs[4:]

    c = pl.program_id(1)
    T, Bb, D = x_ref.shape
    R = T * Bb

    @pl.when(c == 0)
    def _():
        if carried:
            cout_ref[...] = cbuf_ref[...]
        else:
            cout_ref[...] = jnp.zeros_like(cout_ref)

    x1 = x_ref[...].reshape(R, D)
    hbuf[...] = _rms(x1, nf_ref[...]).astype(BF16)
    cols = lambda j: [part * D_FF + j * FF_TILE for part in range(2)]
    up_proj = lambda j: [_dot(hbuf[...], wup_ref[:, col:col + FF_TILE]) for col in cols(j)]
    ups = up_proj(0)
    for j in range(FF_NT):
        ups_next = up_proj(j + 1) if j + 1 < FF_NT else None
        halves = []
        for col, up in zip(cols(j), ups):
            carry = cout_ref[:, :, col:col + FF_TILE].reshape((CONV_W - 1) * Bb, FF_TILE)
            ext = jnp.concatenate([carry, up], axis=0)
            hc = cb_ref[:, col:col + FF_TILE]
            for jj in range(CONV_W):
                hc = hc + cw_ref[jj:jj + 1, col:col + FF_TILE] * ext[jj * Bb:jj * Bb + R]
            cout_ref[:, :, col:col + FF_TILE] = ext[R:R + (CONV_W - 1) * Bb].reshape(CONV_W - 1, Bb, FF_TILE)
            halves.append(hc)
        val, gate = halves
        abuf[:, j * FF_TILE:(j + 1) * FF_TILE] = ((gate * jax.nn.sigmoid(gate)) * val).astype(BF16)
        ups = ups_next
    x2 = x1 + _dot(abuf[...], wd_ref[...])
    if final:
        ys, = refs
        y = _rms(x2, nfin_ref[...])
        for s in range(D // LANES):
            ys[s] = y[:, s * LANES:(s + 1) * LANES]
        for b in range(Bb):
            for s in range(D // LANES):
                x2_ref[b, :, s * LANES:(s + 1) * LANES] = ys[s, pl.ds(b, T, stride=Bb), :]
    else:
        x2_ref[...] = x2.reshape(T, Bb, D)


def _ffn_stage(l, x, W, norm_final, T, Bb, final, cbuf=None, threaded=()):
    L, B, D = x.shape
    R = T * Bb
    carried = cbuf is not None
    act = pl.BlockSpec((T, Bb, D), lambda bb, c: (c, bb, 0))
    lw = lambda shape: _const_spec((None,) + shape, lambda bb, c: (l,) + (0,) * len(shape))
    cb_spec = pl.BlockSpec((None, CONV_W - 1, Bb, 2 * D_FF), lambda bb, c: (l, 0, bb, 0))
    in_specs = [act, lw((1, D)), lw((D, 2 * D_FF)),
                lw((CONV_W, 2 * D_FF)), lw((1, 2 * D_FF)), lw((D_FF, D)),
                _const_spec((1, D), lambda bb, c: (0, 0))]
    args = [x, W["norm_ffn"], W["w_up"], W["conv_w"], W["conv_b"], W["w_down"], norm_final]
    if carried:
        in_specs.append(cb_spec)
        args.append(cbuf)
    aliases = {}
    for prev in threaded:
        in_specs.append(pl.BlockSpec(memory_space=pl.ANY))
        aliases[len(args)] = 1
        args.append(prev)
    scratch = [pltpu.VMEM((R, D), BF16), pltpu.VMEM((R, D_FF), BF16)]
    if final:
        scratch.append(pltpu.VMEM((D // LANES, R, LANES), F32))
        out_spec, out_shape = pl.BlockSpec((Bb, T, D), lambda bb, c: (bb, c, 0)), (B, L, D)
    else:
        out_spec, out_shape = act, (L, B, D)
    return pl.pallas_call(
        functools.partial(_ffn_kernel, carried=carried, final=final, n_threaded=len(threaded)),
        grid=(B // Bb, L // T), in_specs=in_specs, out_specs=[out_spec, cb_spec],
        out_shape=[jax.ShapeDtypeStruct(out_shape, F32),
                   jax.ShapeDtypeStruct((DEPTH, CONV_W - 1, B, 2 * D_FF), F32)],
        scratch_shapes=scratch, input_output_aliases=aliases,
        compiler_params=pltpu.CompilerParams(dimension_semantics=("arbitrary", "arbitrary"),
                                             vmem_limit_bytes=VMEM_LIMIT),
        name="conv_ffn")(*args)


def _prep_weights(norm_mix, w_in, lam_re, lam_im, log_dt, b_re, b_im, c_re, c_im, d, w_glu, w_ssm_out,
                  w_ret_out, w_o, norm_ffn, w_up, conv_w, conv_b, w_down):
    depth = w_in.shape[0]
    ar, ai, bbr, bbi = _s5_prep(lam_re, lam_im, log_dt, b_re, b_im)
    ch_group = jnp.arange(SSM_BLK_CH) // SSM_GROUP
    st_group = jnp.arange(SSM_BLK_STATE) // SSM_STATE
    in_mask = (ch_group[:, None] == st_group[None, :]).astype(F32)

    def in_mat(bb):
        m = jnp.tile(bb.reshape(depth, SSM_NBLK, SSM_BLK_CH, SSM_STATE), (1, 1, 1, SSM_BLK_GROUPS))
        return m * in_mask

    def out_mat(cc):
        m = jnp.transpose(cc, (0, 1, 3, 2)).reshape(depth, SSM_NBLK, SSM_BLK_STATE, SSM_GROUP)
        return jnp.tile(m, (1, 1, 1, SSM_BLK_GROUPS)) * in_mask.T

    lam_rows = lambda a: jnp.broadcast_to(
        a.reshape(depth, SSM_GROUPS, SSM_GROUP, SSM_STATE)[:, :, 0].reshape(depth, SSM_NBLK, 1, SSM_BLK_STATE),
        (depth, SSM_NBLK, SUBLANES, SSM_BLK_STATE))
    return {
        "norm_mix": norm_mix[:, None, :],
        "w_in": w_in.astype(BF16),
        "wb": jnp.concatenate([in_mat(bbr), in_mat(bbi)], axis=-1).astype(BF16),
        "wc": jnp.concatenate([out_mat(c_re), -out_mat(c_im)], axis=2).astype(BF16),
        "are": lam_rows(ar), "aim": lam_rows(ai),
        "d": d[:, None, :],
        "w_glu": w_glu.astype(BF16), "w_ssm_out": w_ssm_out.astype(BF16),
        "w_ret_out": w_ret_out.astype(BF16), "w_o": w_o.astype(BF16),
        "norm_ffn": norm_ffn[:, None, :],
        "w_up": w_up.astype(BF16),
        "conv_w": conv_w, "conv_b": conv_b[:, None, :],
        "w_down": w_down.astype(BF16),
    }


def _rope_tables(pos):
    half = RET_DK // 2
    inv = ROPE_BASE ** (-jnp.arange(half, dtype=F32) / half)
    ang = pos[:, None] * inv[None, :]
    cos = jnp.cos(ang)
    sin = jnp.sin(ang)
    cosf = jnp.concatenate([cos, cos], axis=-1)[:, None, :]
    sinf = jnp.concatenate([-sin, sin], axis=-1)[:, None, :]
    return cosf, sinf


def _ssm_state_in(re, im):
    depth, B = re.shape[:2]
    f = lambda a: jnp.transpose(a.reshape(depth, B, SSM_NBLK, SSM_BLK_STATE), (0, 2, 1, 3))
    return jnp.concatenate([f(re), f(im)], axis=-1)


def _ssm_state_out(st):
    depth, _, B, _ = st.shape
    f = lambda a: jnp.transpose(a, (0, 2, 1, 3)).reshape(depth, B, SSM_GROUPS, SSM_STATE)
    return f(st[..., :SSM_BLK_STATE]), f(st[..., SSM_BLK_STATE:])


def _run_group(x, pos, W, norm_final, mix_tile, ffn_tile, state=None):
    cosf, sinf = _rope_tables(pos)
    swap_tb = lambda a: jnp.transpose(a, (0, 2, 1, 3))
    carried_state, cbuf = None, None
    if state is not None:
        ssm_re, ssm_im, ret0, conv0 = state
        carried_state = (_ssm_state_in(ssm_re, ssm_im), ret0)
        cbuf = swap_tb(conv0)
    xt, mix_acc, conv_acc = x, (), ()
    for l in range(DEPTH):
        xt, *mix_acc = _mix_stage(l, xt, cosf, sinf, W, *mix_tile, x_batch_major=(l == 0),
                                  carried_state=carried_state, threaded=tuple(mix_acc))
        xt, *conv_acc = _ffn_stage(l, xt, W, norm_final[None, :], *ffn_tile, final=(l == DEPTH - 1),
                                   cbuf=cbuf, threaded=tuple(conv_acc))
    st, ret = mix_acc
    sre, sim = _ssm_state_out(st)
    return xt, sre, sim, ret, swap_tb(conv_acc[0])


def kernel(x_prompt, x_sample, state_ssm_re, state_ssm_im, state_ret, state_conv, norm_mix, w_in, ssm_lam_re, ssm_lam_im, ssm_log_dt, ssm_b_re, ssm_b_im, ssm_c_re, ssm_c_im, ssm_d, w_glu, w_ssm_out, w_ret_out, w_o, norm_ffn, w_up, conv_w, conv_b, w_down, norm_final):
    W = _prep_weights(norm_mix, w_in, ssm_lam_re, ssm_lam_im, ssm_log_dt, ssm_b_re, ssm_b_im, ssm_c_re,
                      ssm_c_im, ssm_d, w_glu, w_ssm_out, w_ret_out, w_o, norm_ffn, w_up, conv_w, conv_b, w_down)
    bp, seq, _ = x_prompt.shape
    bs, dseq, _ = x_sample.shape
    tp = min(64, seq)
    yp, rep, imp, retp, convp = _run_group(
        x_prompt, jnp.arange(seq, dtype=F32), W, norm_final, mix_tile=(tp, bp), ffn_tile=(tp, bp))
    ys, res, ims, rets, convs = _run_group(
        x_sample, PAST_LEN + jnp.arange(dseq, dtype=F32), W, norm_final,
        mix_tile=(dseq, min(8, bs)), ffn_tile=(dseq, min(64, bs)),
        state=(state_ssm_re, state_ssm_im, state_ret, state_conv))
    return (yp, ys, rep, imp, retp, convp, res, ims, rets, convs)
```

```python
import functools

import jax
import jax.numpy as jnp
from jax import lax
from jax.experimental import pallas as pl
from jax.experimental.pallas import tpu as pltpu

D_MODEL = 1024
DEPTH = 4
PAST_LEN = 16384
D_SSM = D_MODEL // 2
SSM_GROUP = 16
SSM_GROUPS = D_SSM // SSM_GROUP
SSM_STATE = 64
RET_HEADS = 4
RET_DK = D_MODEL // (2 * RET_HEADS)
RET_DV = 2 * RET_DK
ROPE_BASE = 10000.0
D_FF = ((8 * D_MODEL // 3 + 127) // 128) * 128
CONV_W = 3
EPS = 1e-6
QK_W = RET_HEADS * RET_DK
V_W = RET_HEADS * RET_DV
IN_COLS = D_SSM + 2 * QK_W + 2 * V_W + 2 * D_MODEL
C_U = 0
C_Q = C_U + D_SSM
C_K = C_Q + QK_W
C_V = C_K + QK_W
C_GR = C_V + V_W
C_GA = C_GR + V_W
C_GB = C_GA + D_MODEL

SUBLANES = 8
LANES = 128
SSM_BLK_CH = LANES
SSM_NBLK = D_SSM // SSM_BLK_CH
SSM_BLK_GROUPS = SSM_BLK_CH // SSM_GROUP
SSM_BLK_STATE = SSM_BLK_GROUPS * SSM_STATE
FF_TILE = 256
FF_NT = D_FF // FF_TILE
VMEM_LIMIT = 56 * 1024 * 1024
GATE_TILE = 256
RET_ITEMS_PER_GATE_TILE = 8

BF16 = jnp.bfloat16
F32 = jnp.float32


def _dot(a, b):
    return jnp.dot(a, b, preferred_element_type=F32)


def _rms(x, g):
    ms = jnp.mean(x * x, axis=-1, keepdims=True)
    return x * lax.rsqrt(ms + EPS) * g


def _const_spec(block_shape, index_map):
    return pl.BlockSpec(block_shape, index_map, pipeline_mode=pl.Buffered(1))


def _s5_prep_kernel(lr_ref, li_ref, ldt_ref, br_ref, bi_ref, ar_ref, ai_ref, bbr_ref, bbi_ref):
    lr = lr_ref[...]
    li = li_ref[...]
    dt = jnp.exp(ldt_ref[...])
    mag = jnp.exp(lr * dt)
    ar = mag * jnp.cos(li * dt)
    ai = mag * jnp.sin(li * dt)
    nr = ar - 1.0
    den = lr * lr + li * li
    fr = (nr * lr + ai * li) / den
    fi = (ai * lr - nr * li) / den
    br = br_ref[...]
    bi = bi_ref[...]
    ar_ref[...] = ar
    ai_ref[...] = ai
    bbr_ref[...] = fr * br - fi * bi
    bbi_ref[...] = fr * bi + fi * br


def _s5_prep(lam_re, lam_im, log_dt, b_re, b_im):
    depth = lam_re.shape[0]
    rows = SSM_GROUPS * SSM_GROUP
    rep = lambda a: jnp.repeat(a, SSM_GROUP, axis=1)
    lr = rep(lam_re)
    li = rep(lam_im)
    ldt = rep(jnp.broadcast_to(log_dt[:, :, None], lam_re.shape))
    tr = lambda b: jnp.transpose(b, (0, 1, 3, 2)).reshape(depth, rows, SSM_STATE)
    spec = pl.BlockSpec((None, rows, SSM_STATE), lambda l: (l, 0, 0))
    shp = jax.ShapeDtypeStruct((depth, rows, SSM_STATE), F32)
    ar, ai, bbr, bbi = pl.pallas_call(
        _s5_prep_kernel, grid=(depth,), in_specs=[spec] * 5, out_specs=[spec] * 4,
        out_shape=[shp] * 4, name="s5_prep")(lr, li, ldt, tr(b_re), tr(b_im))
    return ar, ai, bbr, bbi


def _mix_kernel(*refs, x_batch_major, carried, n_threaded):
    refs = list(refs)
    (x_ref, nrm_ref, win_ref, wb_ref, wc_ref, are_ref, aim_ref, d_ref, wglu_ref, wso_ref, wro_ref, wo_ref,
     cos_ref, sin_ref, intra_ref, inner_ref, tail_ref, decay_ref) = refs[:18]
    refs = refs[18:]
    if carried:
        s0_ref, r0_ref = refs[:2]
        refs = refs[2:]
    refs = refs[n_threaded:]
    x1_ref, st_ref, r_ref, hbuf, sbuf, ybuf, mixbuf, qs, ks, vs, os_, gsil, sgb = refs[:13]
    refs = refs[13:]

    c = pl.program_id(1)
    if x_batch_major:
        Bb, T, D = x_ref.shape
    else:
        T, Bb, D = x_ref.shape
    R = T * Bb
    P2 = SSM_BLK_STATE
    NS = D // LANES

    @pl.when(c == 0)
    def _():
        if carried:
            st_ref[...] = s0_ref[...]
            r_ref[...] = r0_ref[...]
        else:
            st_ref[...] = jnp.zeros_like(st_ref)
            r_ref[...] = jnp.zeros_like(r_ref)

    if x_batch_major:
        xs, = refs
        for b in range(Bb):
            for s in range(NS):
                xs[s, pl.ds(b, T, stride=Bb), :] = x_ref[b, :, s * LANES:(s + 1) * LANES]
        load_x = lambda: jnp.concatenate([xs[s] for s in range(NS)], axis=1)
    else:
        load_x = lambda: x_ref[...].reshape(R, D)

    hbuf[...] = _rms(load_x(), nrm_ref[...]).astype(BF16)

    def project_qk():
        cosf = cos_ref[...]
        sinf = sin_ref[...]
        for (c0, slab, scale) in ((C_Q, qs, None), (C_K, ks, RET_DK ** -0.5)):
            z = _dot(hbuf[...], win_ref[:, c0:c0 + QK_W])
            for hd in range(RET_HEADS):
                zh = z[:, hd * RET_DK:(hd + 1) * RET_DK]
                zr = pltpu.roll(zh, RET_DK // 2, 1)
                rot = zh.reshape(T, Bb, RET_DK) * cosf + zr.reshape(T, Bb, RET_DK) * sinf
                if scale is not None:
                    rot = rot * scale
                slab[hd] = rot.reshape(R, RET_DK)

    def project_v():
        v = _dot(hbuf[...], win_ref[:, C_V:C_V + V_W])
        for s in range(V_W // LANES):
            vs[s] = v[:, s * LANES:(s + 1) * LANES]

    u = _dot(hbuf[...], win_ref[:, C_U:C_U + D_SSM])
    ub = u.astype(BF16)
    for i0, side_work in zip(range(0, SSM_NBLK, 2), (project_qk, project_v)):
        blocks = (i0, i0 + 1)
        for j, i in enumerate(blocks):
            sbuf[j] = _dot(ub[:, i * SSM_BLK_CH:(i + 1) * SSM_BLK_CH], wb_ref[i])
        side_work()
        chains = [(j, i, slice(b0, b0 + SUBLANES)) for j, i in enumerate(blocks) for b0 in range(0, Bb, SUBLANES)]
        state = [(st_ref[i, bt, 0:P2], st_ref[i, bt, P2:2 * P2]) for j, i, bt in chains]
        for t in range(T):
            for n, (j, i, bt) in enumerate(chains):
                rows_t = slice(t * Bb + bt.start, t * Bb + bt.stop)
                sre, sim = state[n]
                are = are_ref[i]
                aim = aim_ref[i]
                nre = are * sre - aim * sim + sbuf[j, rows_t, 0:P2]
                nim = are * sim + aim * sre + sbuf[j, rows_t, P2:2 * P2]
                sbuf[j, rows_t, 0:P2] = nre
                sbuf[j, rows_t, P2:2 * P2] = nim
                state[n] = (nre, nim)
        for n, (j, i, bt) in enumerate(chains):
            st_ref[i, bt, 0:P2] = state[n][0]
            st_ref[i, bt, P2:2 * P2] = state[n][1]
        for j, i in enumerate(blocks):
            ybuf[:, i * SSM_BLK_CH:(i + 1) * SSM_BLK_CH] = _dot(sbuf[j].astype(BF16), wc_ref[i])
    y = ybuf[...] + d_ref[...] * u
    ya = jax.nn.gelu(y)
    ya = ya * jax.nn.sigmoid(_dot(ya.astype(BF16), wglu_ref[...]))
    yap = _dot(ya.astype(BF16), wso_ref[...])
    mixbuf[...] = yap

    def gate_tile(kind, n):
        cols = slice(n * GATE_TILE, (n + 1) * GATE_TILE)
        c0 = {"a": C_GA, "r": C_GR, "b": C_GB}[kind] + n * GATE_TILE

        def run():
            z = _dot(hbuf[...], win_ref[:, c0:c0 + GATE_TILE])
            if kind == "a":
                mixbuf[:, cols] = jax.nn.sigmoid(z) * mixbuf[:, cols]
            elif kind == "r":
                gsil[:, cols] = z * jax.nn.sigmoid(z)
            else:
                sgb[:, cols] = jax.nn.sigmoid(z)
        return run

    side = iter([gate_tile(kind, n) for kind in "arb" for n in range(D_MODEL // GATE_TILE)])

    def side_work(idx):
        if (idx + 1) % RET_ITEMS_PER_GATE_TILE == 0:
            run = next(side, None)
            if run is not None:
                run()

    items = [(b, hd) for b in range(Bb) for hd in range(RET_HEADS)]
    rows = lambda b: pl.ds(b, T, stride=Bb)
    qbs, kfs, scs, vbs = [], [], [], []
    for idx, (b, hd) in enumerate(items):
        qb = qs[hd, rows(b), :].astype(BF16)
        kf = ks[hd, rows(b), :]
        qbs.append(qb)
        kfs.append(kf)
        scs.append(lax.dot_general(qb, kf.astype(BF16), (((1,), (1,)), ((), ())), preferred_element_type=F32))
        side_work(idx)
    for idx, (b, hd) in enumerate(items):
        sc = (scs[idx] * intra_ref[hd]).astype(BF16)
        vb = jnp.concatenate([vs[2 * hd, rows(b), :], vs[2 * hd + 1, rows(b), :]], axis=1).astype(BF16)
        vbs.append(vb)
        o = _dot(sc, vb) + _dot(qbs[idx], r_ref[b, hd].astype(BF16)) * inner_ref[hd]
        os_[2 * hd, rows(b), :] = o[:, :LANES]
        os_[2 * hd + 1, rows(b), :] = o[:, LANES:]
        side_work(idx)
    for idx, (b, hd) in enumerate(items):
        kt = (kfs[idx] * tail_ref[hd]).astype(BF16)
        upd = lax.dot_general(kt, vbs[idx], (((0,), (0,)), ((), ())), preferred_element_type=F32)
        r_ref[b, hd] = r_ref[b, hd] * decay_ref[hd] + upd
        side_work(idx)
    for run in side:
        run()

    on_heads = []
    for hd in range(RET_HEADS):
        o = jnp.concatenate([os_[2 * hd], os_[2 * hd + 1]], axis=1)
        on_heads.append(o * lax.rsqrt(jnp.mean(o * o, axis=-1, keepdims=True) + EPS))
    on_all = jnp.concatenate(on_heads, axis=1)
    og = gsil[...] * on_all
    yb = _dot(og.astype(BF16), wro_ref[...])
    mix = mixbuf[...] + sgb[...] * yb
    x1 = load_x() + _dot(mix.astype(BF16), wo_ref[...])
    x1_ref[...] = x1.reshape(T, Bb, D)


def _ret_tables(Tc):
    hidx = jnp.arange(RET_HEADS, dtype=F32)
    log_g = jnp.log1p(-jnp.exp2(-5.0 - hidx))
    idx = jnp.arange(Tc, dtype=F32)
    rel = idx[:, None] - idx[None, :]
    intra = jnp.where(rel >= 0, jnp.exp(jnp.maximum(rel, 0.0)[None] * log_g[:, None, None]), 0.0)
    inner = jnp.exp((idx[None, :] + 1.0) * log_g[:, None])[:, :, None]
    tail = jnp.exp((Tc - 1.0 - idx[None, :]) * log_g[:, None])[:, :, None]
    decay = jnp.exp(Tc * log_g)
    return (intra, jnp.broadcast_to(inner, (RET_HEADS, Tc, RET_DV)),
            jnp.broadcast_to(tail, (RET_HEADS, Tc, RET_DK)), decay)


def _mix_stage(l, x, cosf, sinf, W, T, Bb, x_batch_major, carried_state=None, threaded=()):
    if x_batch_major:
        B, L, D = x.shape
    else:
        L, B, D = x.shape
    carried = carried_state is not None
    R = T * Bb
    intra, inner, tail, decay = _ret_tables(T)
    act = pl.BlockSpec((T, Bb, D), lambda bb, c: (c, bb, 0))
    lw = lambda shape: _const_spec((None,) + shape, lambda bb, c: (l,) + (0,) * len(shape))
    cst = lambda shape: _const_spec(shape, lambda bb, c: (0,) * len(shape))
    rope = pl.BlockSpec((T, 1, RET_DK), lambda bb, c: (c, 0, 0))
    st_spec = pl.BlockSpec((None, SSM_NBLK, Bb, 2 * SSM_BLK_STATE), lambda bb, c: (l, 0, bb, 0))
    r_spec = pl.BlockSpec((None, Bb, RET_HEADS, RET_DK, RET_DV), lambda bb, c: (l, bb, 0, 0, 0))
    x_spec = pl.BlockSpec((Bb, T, D), lambda bb, c: (bb, c, 0)) if x_batch_major else act
    in_specs = [
        x_spec, lw((1, D)), lw((D, IN_COLS)),
        lw((SSM_NBLK, SSM_BLK_CH, 2 * SSM_BLK_STATE)), lw((SSM_NBLK, 2 * SSM_BLK_STATE, SSM_BLK_CH)),
        lw((SSM_NBLK, SUBLANES, SSM_BLK_STATE)), lw((SSM_NBLK, SUBLANES, SSM_BLK_STATE)),
        lw((1, D_SSM)), lw((D_SSM, D_SSM)), lw((D_SSM, D)), lw((V_W, D)), lw((D, D)),
        rope, rope,
        cst((RET_HEADS, T, T)), cst((RET_HEADS, T, RET_DV)), cst((RET_HEADS, T, RET_DK)),
        pl.BlockSpec(memory_space=pltpu.SMEM),
    ]
    args = [x, W["norm_mix"], W["w_in"], W["wb"], W["wc"], W["are"], W["aim"], W["d"],
            W["w_glu"], W["w_ssm_out"], W["w_ret_out"], W["w_o"], cosf, sinf, intra, inner, tail, decay]
    if carried:
        in_specs += [st_spec, r_spec]
        args += list(carried_state)
    aliases = {}
    for k, prev in enumerate(threaded):
        in_specs.append(pl.BlockSpec(memory_space=pl.ANY))
        aliases[len(args)] = 1 + k
        args.append(prev)
    slab = lambda n: pltpu.VMEM((n, R, LANES), F32)
    scratch = [pltpu.VMEM((R, D), BF16), pltpu.VMEM((2, R, 2 * SSM_BLK_STATE), F32),
               pltpu.VMEM((R, D_SSM), F32), pltpu.VMEM((R, D), F32),
               slab(QK_W // LANES), slab(QK_W // LANES), slab(V_W // LANES), slab(V_W // LANES),
               pltpu.VMEM((R, V_W), F32), pltpu.VMEM((R, D), F32)]
    if x_batch_major:
        scratch.append(slab(D // LANES))
    return pl.pallas_call(
        functools.partial(_mix_kernel, x_batch_major=x_batch_major, carried=carried, n_threaded=len(threaded)),
        grid=(B // Bb, L // T), in_specs=in_specs, out_specs=[act, st_spec, r_spec],
        out_shape=[jax.ShapeDtypeStruct((L, B, D), F32),
                   jax.ShapeDtypeStruct((DEPTH, SSM_NBLK, B, 2 * SSM_BLK_STATE), F32),
                   jax.ShapeDtypeStruct((DEPTH, B, RET_HEADS, RET_DK, RET_DV), F32)],
        scratch_shapes=scratch, input_output_aliases=aliases,
        compiler_params=pltpu.CompilerParams(dimension_semantics=("arbitrary", "arbitrary"),
                                             vmem_limit_bytes=VMEM_LIMIT),
        name="mixer_fused")(*args)


def _ffn_kernel(*refs, carried, final, n_threaded):
    refs = list(refs)
    x_ref, nf_ref, wup_ref, cw_ref, cb_ref, wd_ref, nfin_ref = refs[:7]
    refs = refs[7:]
    if carried:
        cbuf_ref = refs[0]
        refs = refs[1:]
    refs = refs[n_threaded:]
    x2_ref, cout_ref, hbuf, abuf = refs[:4]
    refs = refs[4:]

    c = pl.program_id(1)
    T, Bb, D = x_ref.shape
    R = T * Bb

    @pl.when(c == 0)
    def _():
        if carried:
            cout_ref[...] = cbuf_ref[...]
        else:
            cout_ref[...] = jnp.zeros_like(cout_ref)

    x1 = x_ref[...].reshape(R, D)
    hbuf[...] = _rms(x1, nf_ref[...]).astype(BF16)
    cols = lambda j: [part * D_FF + j * FF_TILE for part in range(2)]
    up_proj = lambda j: [_dot(hbuf[...], wup_ref[:, col:col + FF_TILE]) for col in cols(j)]
    ups = up_proj(0)
    for j in range(FF_NT):
        ups_next = up_proj(j + 1) if j + 1 < FF_NT else None
        halves = []
        for col, up in zip(cols(j), ups):
            carry = cout_ref[:, :, col:col + FF_TILE].reshape((CONV_W - 1) * Bb, FF_TILE)
            ext = jnp.concatenate([carry, up], axis=0)
            hc = cb_ref[:, col:col + FF_TILE]
            for jj in range(CONV_W):
                hc = hc + cw_ref[jj:jj + 1, col:col + FF_TILE] * ext[jj * Bb:jj * Bb + R]
            cout_ref[:, :, col:col + FF_TILE] = ext[R:R + (CONV_W - 1) * Bb].reshape(CONV_W - 1, Bb, FF_TILE)
            halves.append(hc)
        val, gate = halves
        abuf[:, j * FF_TILE:(j + 1) * FF_TILE] = ((gate * jax.nn.sigmoid(gate)) * val).astype(BF16)
        ups = ups_next
    x2 = x1 + _dot(abuf[...], wd_ref[...])
    if final:
        ys, = refs
        y = _rms(x2, nfin_ref[...])
        for s in range(D // LANES):
            ys[s] = y[:, s * LANES:(s + 1) * LANES]
        for b in range(Bb):
            for s in range(D // LANES):
                x2_ref[b, :, s * LANES:(s + 1) * LANES] = ys[s, pl.ds(b, T, stride=Bb), :]
    else:
        x2_ref[...] = x2.reshape(T, Bb, D)


def _ffn_stage(l, x, W, norm_final, T, Bb, final, cbuf=None, threaded=()):
    L, B, D = x.shape
    R = T * Bb
    carried = cbuf is not None
    act = pl.BlockSpec((T, Bb, D), lambda bb, c: (c, bb, 0))
    lw = lambda shape: _const_spec((None,) + shape, lambda bb, c: (l,) + (0,) * len(shape))
    cb_spec = pl.BlockSpec((None, CONV_W - 1, Bb, 2 * D_FF), lambda bb, c: (l, 0, bb, 0))
    in_specs = [act, lw((1, D)), lw((D, 2 * D_FF)),
                lw((CONV_W, 2 * D_FF)), lw((1, 2 * D_FF)), lw((D_FF, D)),
                _const_spec((1, D), lambda bb, c: (0, 0))]
    args = [x, W["norm_ffn"], W["w_up"], W["conv_w"], W["conv_b"], W["w_down"], norm_final]
    if carried:
        in_specs.append(cb_spec)
        args.append(cbuf)
    aliases = {}
    for prev in threaded:
        in_specs.append(pl.BlockSpec(memory_space=pl.ANY))
        aliases[len(args)] = 1
        args.append(prev)
    scratch = [pltpu.VMEM((R, D), BF16), pltpu.VMEM((R, D_FF), BF16)]
    if final:
        scratch.append(pltpu.VMEM((D // LANES, R, LANES), F32))
        out_spec, out_shape = pl.BlockSpec((Bb, T, D), lambda bb, c: (bb, c, 0)), (B, L, D)
    else:
        out_spec, out_shape = act, (L, B, D)
    return pl.pallas_call(
        functools.partial(_ffn_kernel, carried=carried, final=final, n_threaded=len(threaded)),
        grid=(B // Bb, L // T), in_specs=in_specs, out_specs=[out_spec, cb_spec],
        out_shape=[jax.ShapeDtypeStruct(out_shape, F32),
                   jax.ShapeDtypeStruct((DEPTH, CONV_W - 1, B, 2 * D_FF), F32)],
        scratch_shapes=scratch, input_output_aliases=aliases,
        compiler_params=pltpu.CompilerParams(dimension_semantics=("arbitrary", "arbitrary"),
                                             vmem_limit_bytes=VMEM_LIMIT),
        name="conv_ffn")(*args)


def _prep_weights(norm_mix, w_in, lam_re, lam_im, log_dt, b_re, b_im, c_re, c_im, d, w_glu, w_ssm_out,
                  w_ret_out, w_o, norm_ffn, w_up, conv_w, conv_b, w_down):
    depth = w_in.shape[0]
    ar, ai, bbr, bbi = _s5_prep(lam_re, lam_im, log_dt, b_re, b_im)
    ch_group = jnp.arange(SSM_BLK_CH) // SSM_GROUP
    st_group = jnp.arange(SSM_BLK_STATE) // SSM_STATE
    in_mask = (ch_group[:, None] == st_group[None, :]).astype(F32)

    def in_mat(bb):
        m = jnp.tile(bb.reshape(depth, SSM_NBLK, SSM_BLK_CH, SSM_STATE), (1, 1, 1, SSM_BLK_GROUPS))
        return m * in_mask

    def out_mat(cc):
        m = jnp.transpose(cc, (0, 1, 3, 2)).reshape(depth, SSM_NBLK, SSM_BLK_STATE, SSM_GROUP)
        return jnp.tile(m, (1, 1, 1, SSM_BLK_GROUPS)) * in_mask.T

    lam_rows = lambda a: jnp.broadcast_to(
        a.reshape(depth, SSM_GROUPS, SSM_GROUP, SSM_STATE)[:, :, 0].reshape(depth, SSM_NBLK, 1, SSM_BLK_STATE),
        (depth, SSM_NBLK, SUBLANES, SSM_BLK_STATE))
    return {
        "norm_mix": norm_mix[:, None, :],
        "w_in": w_in.astype(BF16),
        "wb": jnp.concatenate([in_mat(bbr), in_mat(bbi)], axis=-1).astype(BF16),
        "wc": jnp.concatenate([out_mat(c_re), -out_mat(c_im)], axis=2).astype(BF16),
        "are": lam_rows(ar), "aim": lam_rows(ai),
        "d": d[:, None, :],
        "w_glu": w_glu.astype(BF16), "w_ssm_out": w_ssm_out.astype(BF16),
        "w_ret_out": w_ret_out.astype(BF16), "w_o": w_o.astype(BF16),
        "norm_ffn": norm_ffn[:, None, :],
        "w_up": w_up.astype(BF16),
        "conv_w": conv_w, "conv_b": conv_b[:, None, :],
        "w_down": w_down.astype(BF16),
    }


def _rope_tables(pos):
    half = RET_DK // 2
    inv = ROPE_BASE ** (-jnp.arange(half, dtype=F32) / half)
    ang = pos[:, None] * inv[None, :]
    cos = jnp.cos(ang)
    sin = jnp.sin(ang)
    cosf = jnp.concatenate([cos, cos], axis=-1)[:, None, :]
    sinf = jnp.concatenate([-sin, sin], axis=-1)[:, None, :]
    return cosf, sinf


def _ssm_state_in(re, im):
    depth, B = re.shape[:2]
    f = lambda a: jnp.transpose(a.reshape(depth, B, SSM_NBLK, SSM_BLK_STATE), (0, 2, 1, 3))
    return jnp.concatenate([f(re), f(im)], axis=-1)


def _ssm_state_out(st):
    depth, _, B, _ = st.shape
    f = lambda a: jnp.transpose(a, (0, 2, 1, 3)).reshape(depth, B, SSM_GROUPS, SSM_STATE)
    return f(st[..., :SSM_BLK_STATE]), f(st[..., SSM_BLK_STATE:])


def _run_group(x, pos, W, norm_final, mix_tile, ffn_tile, state=None):
    cosf, sinf = _rope_tables(pos)
    swap_tb = lambda a: jnp.transpose(a, (0, 2, 1, 3))
    carried_state, cbuf = None, None
    if state is not None:
        ssm_re, ssm_im, ret0, conv0 = state
        carried_state = (_ssm_state_in(ssm_re, ssm_im), ret0)
        cbuf = swap_tb(conv0)
    xt, mix_acc, conv_acc = x, (), ()
    for l in range(DEPTH):
        xt, *mix_acc = _mix_stage(l, xt, cosf, sinf, W, *mix_tile, x_batch_major=(l == 0),
                                  carried_state=carried_state, threaded=tuple(mix_acc))
        xt, *conv_acc = _ffn_stage(l, xt, W, norm_final[None, :], *ffn_tile, final=(l == DEPTH - 1),
                                   cbuf=cbuf, threaded=tuple(conv_acc))
    st, ret = mix_acc
    sre, sim = _ssm_state_out(st)
    return xt, sre, sim, ret, swap_tb(conv_acc[0])


def kernel(x_prompt, x_sample, state_ssm_re, state_ssm_im, state_ret, state_conv, norm_mix, w_in, ssm_lam_re, ssm_lam_im, ssm_log_dt, ssm_b_re, ssm_b_im, ssm_c_re, ssm_c_im, ssm_d, w_glu, w_ssm_out, w_ret_out, w_o, norm_ffn, w_up, conv_w, conv_b, w_down, norm_final):
    W = _prep_weights(norm_mix, w_in, ssm_lam_re, ssm_lam_im, ssm_log_dt, ssm_b_re, ssm_b_im, ssm_c_re,
                      ssm_c_im, ssm_d, w_glu, w_ssm_out, w_ret_out, w_o, norm_ffn, w_up, conv_w, conv_b, w_down)
    bp, seq, _ = x_prompt.shape
    bs, dseq, _ = x_sample.shape
    tp = min(64, seq)
    yp, rep, imp, retp, convp = _run_group(
        x_prompt, jnp.arange(seq, dtype=F32), W, norm_final, mix_tile=(tp, bp), ffn_tile=(min(128, seq), bp))
    ys, res, ims, rets, convs = _run_group(
        x_sample, PAST_LEN + jnp.arange(dseq, dtype=F32), W, norm_final,
        mix_tile=(dseq, min(8, bs)), ffn_tile=(dseq, min(64, bs)),
        state=(state_ssm_re, state_ssm_im, state_ret, state_conv))
    return (yp, ys, rep, imp, retp, convp, res, ims, rets, convs)
```

```python
import functools

import jax
import jax.numpy as jnp
from jax import lax
from jax.experimental import pallas as pl
from jax.experimental.pallas import tpu as pltpu

D_MODEL = 1024
DEPTH = 4
PAST_LEN = 16384
D_SSM = D_MODEL // 2
SSM_GROUP = 16
SSM_GROUPS = D_SSM // SSM_GROUP
SSM_STATE = 64
RET_HEADS = 4
RET_DK = D_MODEL // (2 * RET_HEADS)
RET_DV = 2 * RET_DK
ROPE_BASE = 10000.0
D_FF = ((8 * D_MODEL // 3 + 127) // 128) * 128
CONV_W = 3
EPS = 1e-6
QK_W = RET_HEADS * RET_DK
V_W = RET_HEADS * RET_DV
IN_COLS = D_SSM + 2 * QK_W + 2 * V_W + 2 * D_MODEL
C_U = 0
C_Q = C_U + D_SSM
C_K = C_Q + QK_W
C_V = C_K + QK_W
C_GR = C_V + V_W
C_GA = C_GR + V_W
C_GB = C_GA + D_MODEL

SUBLANES = 8
LANES = 128
SSM_BLK_CH = LANES
SSM_NBLK = D_SSM // SSM_BLK_CH
SSM_BLK_GROUPS = SSM_BLK_CH // SSM_GROUP
SSM_BLK_STATE = SSM_BLK_GROUPS * SSM_STATE
FF_TILE = 256
FF_NT = D_FF // FF_TILE
VMEM_LIMIT = 56 * 1024 * 1024
GATE_TILE = 256
RET_ITEMS_PER_GATE_TILE = 8

BF16 = jnp.bfloat16
F32 = jnp.float32


def _dot(a, b):
    return jnp.dot(a, b, preferred_element_type=F32)


def _rms(x, g):
    ms = jnp.mean(x * x, axis=-1, keepdims=True)
    return x * lax.rsqrt(ms + EPS) * g


def _const_spec(block_shape, index_map):
    return pl.BlockSpec(block_shape, index_map, pipeline_mode=pl.Buffered(1))


def _s5_prep_kernel(lr_ref, li_ref, ldt_ref, br_ref, bi_ref, ar_ref, ai_ref, bbr_ref, bbi_ref):
    lr = lr_ref[...]
    li = li_ref[...]
    dt = jnp.exp(ldt_ref[...])
    mag = jnp.exp(lr * dt)
    ar = mag * jnp.cos(li * dt)
    ai = mag * jnp.sin(li * dt)
    nr = ar - 1.0
    den = lr * lr + li * li
    fr = (nr * lr + ai * li) / den
    fi = (ai * lr - nr * li) / den
    br = br_ref[...]
    bi = bi_ref[...]
    ar_ref[...] = ar
    ai_ref[...] = ai
    bbr_ref[...] = fr * br - fi * bi
    bbi_ref[...] = fr * bi + fi * br


def _s5_prep(lam_re, lam_im, log_dt, b_re, b_im):
    depth = lam_re.shape[0]
    rows = SSM_GROUPS * SSM_GROUP
    rep = lambda a: jnp.repeat(a, SSM_GROUP, axis=1)
    lr = rep(lam_re)
    li = rep(lam_im)
    ldt = rep(jnp.broadcast_to(log_dt[:, :, None], lam_re.shape))
    tr = lambda b: jnp.transpose(b, (0, 1, 3, 2)).reshape(depth, rows, SSM_STATE)
    spec = pl.BlockSpec((None, rows, SSM_STATE), lambda l: (l, 0, 0))
    shp = jax.ShapeDtypeStruct((depth, rows, SSM_STATE), F32)
    ar, ai, bbr, bbi = pl.pallas_call(
        _s5_prep_kernel, grid=(depth,), in_specs=[spec] * 5, out_specs=[spec] * 4,
        out_shape=[shp] * 4, name="s5_prep")(lr, li, ldt, tr(b_re), tr(b_im))
    return ar, ai, bbr, bbi


def _mix_kernel(*refs, x_batch_major, carried, n_threaded):
    refs = list(refs)
    (x_ref, nrm_ref, win_ref, wb_ref, wc_ref, are_ref, aim_ref, d_ref, wglu_ref, wso_ref, wro_ref, wo_ref,
     cos_ref, sin_ref, intra_ref, inner_ref, tail_ref, decay_ref) = refs[:18]
    refs = refs[18:]
    if carried:
        s0_ref, r0_ref = refs[:2]
        refs = refs[2:]
    refs = refs[n_threaded:]
    x1_ref, st_ref, r_ref, hbuf, sbuf, ybuf, mixbuf, qs, ks, vs, os_, gsil, sgb = refs[:13]
    refs = refs[13:]

    c = pl.program_id(1)
    if x_batch_major:
        Bb, T, D = x_ref.shape
    else:
        T, Bb, D = x_ref.shape
    R = T * Bb
    P2 = SSM_BLK_STATE
    NS = D // LANES

    @pl.when(c == 0)
    def _():
        if carried:
            st_ref[...] = s0_ref[...]
            r_ref[...] = r0_ref[...]
        else:
            st_ref[...] = jnp.zeros_like(st_ref)
            r_ref[...] = jnp.zeros_like(r_ref)

    if x_batch_major:
        xs, = refs
        for b in range(Bb):
            for s in range(NS):
                xs[s, pl.ds(b, T, stride=Bb), :] = x_ref[b, :, s * LANES:(s + 1) * LANES]
        load_x = lambda: jnp.concatenate([xs[s] for s in range(NS)], axis=1)
    else:
        load_x = lambda: x_ref[...].reshape(R, D)

    hbuf[...] = _rms(load_x(), nrm_ref[...]).astype(BF16)

    def project_qk():
        cosf = cos_ref[...]
        sinf = sin_ref[...]
        for (c0, slab, scale) in ((C_Q, qs, None), (C_K, ks, RET_DK ** -0.5)):
            z = _dot(hbuf[...], win_ref[:, c0:c0 + QK_W])
            for hd in range(RET_HEADS):
                zh = z[:, hd * RET_DK:(hd + 1) * RET_DK]
                zr = pltpu.roll(zh, RET_DK // 2, 1)
                rot = zh.reshape(T, Bb, RET_DK) * cosf + zr.reshape(T, Bb, RET_DK) * sinf
                if scale is not None:
                    rot = rot * scale
                slab[hd] = rot.reshape(R, RET_DK)

    def project_v():
        v = _dot(hbuf[...], win_ref[:, C_V:C_V + V_W])
        for s in range(V_W // LANES):
            vs[s] = v[:, s * LANES:(s + 1) * LANES]

    u = _dot(hbuf[...], win_ref[:, C_U:C_U + D_SSM])
    ub = u.astype(BF16)
    for i0, side_work in zip(range(0, SSM_NBLK, 2), (project_qk, project_v)):
        blocks = (i0, i0 + 1)
        for j, i in enumerate(blocks):
            sbuf[j] = _dot(ub[:, i * SSM_BLK_CH:(i + 1) * SSM_BLK_CH], wb_ref[i])
        side_work()
        chains = [(j, i, slice(b0, b0 + SUBLANES)) for j, i in enumerate(blocks) for b0 in range(0, Bb, SUBLANES)]
        state = [(st_ref[i, bt, 0:P2], st_ref[i, bt, P2:2 * P2]) for j, i, bt in chains]
        for t in range(T):
            for n, (j, i, bt) in enumerate(chains):
                rows_t = slice(t * Bb + bt.start, t * Bb + bt.stop)
                sre, sim = state[n]
                are = are_ref[i]
                aim = aim_ref[i]
                nre = are * sre - aim * sim + sbuf[j, rows_t, 0:P2]
                nim = are * sim + aim * sre + sbuf[j, rows_t, P2:2 * P2]
                sbuf[j, rows_t, 0:P2] = nre
                sbuf[j, rows_t, P2:2 * P2] = nim
                state[n] = (nre, nim)
        for n, (j, i, bt) in enumerate(chains):
            st_ref[i, bt, 0:P2] = state[n][0]
            st_ref[i, bt, P2:2 * P2] = state[n][1]
        for j, i in enumerate(blocks):
            ybuf[:, i * SSM_BLK_CH:(i + 1) * SSM_BLK_CH] = _dot(sbuf[j].astype(BF16), wc_ref[i])
    y = ybuf[...] + d_ref[...] * u
    ya = jax.nn.gelu(y)
    ya = ya * jax.nn.sigmoid(_dot(ya.astype(BF16), wglu_ref[...]))
    yap = _dot(ya.astype(BF16), wso_ref[...])
    mixbuf[...] = yap

    def gate_tile(kind, n):
        cols = slice(n * GATE_TILE, (n + 1) * GATE_TILE)
        c0 = {"a": C_GA, "r": C_GR, "b": C_GB}[kind] + n * GATE_TILE

        def run():
            z = _dot(hbuf[...], win_ref[:, c0:c0 + GATE_TILE])
            if kind == "a":
                mixbuf[:, cols] = jax.nn.sigmoid(z) * mixbuf[:, cols]
            elif kind == "r":
                gsil[:, cols] = z * jax.nn.sigmoid(z)
            else:
                sgb[:, cols] = jax.nn.sigmoid(z)
        return run

    side = iter([gate_tile(kind, n) for kind in "arb" for n in range(D_MODEL // GATE_TILE)])

    def side_work(idx):
        if (idx + 1) % RET_ITEMS_PER_GATE_TILE == 0:
            run = next(side, None)
            if run is not None:
                run()

    items = [(b, hd) for b in range(Bb) for hd in range(RET_HEADS)]
    rows = lambda b: pl.ds(b, T, stride=Bb)
    qbs, kfs, scs, vbs = [], [], [], []
    for idx, (b, hd) in enumerate(items):
        qb = qs[hd, rows(b), :].astype(BF16)
        kf = ks[hd, rows(b), :]
        qbs.append(qb)
        kfs.append(kf)
        scs.append(lax.dot_general(qb, kf.astype(BF16), (((1,), (1,)), ((), ())), preferred_element_type=F32))
        side_work(idx)
    for idx, (b, hd) in enumerate(items):
        sc = (scs[idx] * intra_ref[hd]).astype(BF16)
        vb = jnp.concatenate([vs[2 * hd, rows(b), :], vs[2 * hd + 1, rows(b), :]], axis=1).astype(BF16)
        vbs.append(vb)
        o = _dot(sc, vb) + _dot(qbs[idx], r_ref[b, hd].astype(BF16)) * inner_ref[hd]
        os_[2 * hd, rows(b), :] = o[:, :LANES]
        os_[2 * hd + 1, rows(b), :] = o[:, LANES:]
        side_work(idx)
    for idx, (b, hd) in enumerate(items):
        kt = (kfs[idx] * tail_ref[hd]).astype(BF16)
        upd = lax.dot_general(kt, vbs[idx], (((0,), (0,)), ((), ())), preferred_element_type=F32)
        r_ref[b, hd] = r_ref[b, hd] * decay_ref[hd] + upd
        side_work(idx)
    for run in side:
        run()

    on_heads = []
    for hd in range(RET_HEADS):
        o = jnp.concatenate([os_[2 * hd], os_[2 * hd + 1]], axis=1)
        on_heads.append(o * lax.rsqrt(jnp.mean(o * o, axis=-1, keepdims=True) + EPS))
    on_all = jnp.concatenate(on_heads, axis=1)
    og = gsil[...] * on_all
    yb = _dot(og.astype(BF16), wro_ref[...])
    mix = mixbuf[...] + sgb[...] * yb
    x1 = load_x() + _dot(mix.astype(BF16), wo_ref[...])
    x1_ref[...] = x1.reshape(T, Bb, D)


def _ret_tables(Tc):
    hidx = jnp.arange(RET_HEADS, dtype=F32)
    log_g = jnp.log1p(-jnp.exp2(-5.0 - hidx))
    idx = jnp.arange(Tc, dtype=F32)
    rel = idx[:, None] - idx[None, :]
    intra = jnp.where(rel >= 0, jnp.exp(jnp.maximum(rel, 0.0)[None] * log_g[:, None, None]), 0.0)
    inner = jnp.exp((idx[None, :] + 1.0) * log_g[:, None])[:, :, None]
    tail = jnp.exp((Tc - 1.0 - idx[None, :]) * log_g[:, None])[:, :, None]
    decay = jnp.exp(Tc * log_g)
    return (intra, jnp.broadcast_to(inner, (RET_HEADS, Tc, RET_DV)),
            jnp.broadcast_to(tail, (RET_HEADS, Tc, RET_DK)), decay)


def _mix_stage(l, x, cosf, sinf, W, T, Bb, x_batch_major, carried_state=None, threaded=()):
    if x_batch_major:
        B, L, D = x.shape
    else:
        L, B, D = x.shape
    carried = carried_state is not None
    R = T * Bb
    intra, inner, tail, decay = _ret_tables(T)
    act = pl.BlockSpec((T, Bb, D), lambda bb, c: (c, bb, 0))
    lw = lambda shape: _const_spec((None,) + shape, lambda bb, c: (l,) + (0,) * len(shape))
    cst = lambda shape: _const_spec(shape, lambda bb, c: (0,) * len(shape))
    rope = pl.BlockSpec((T, 1, RET_DK), lambda bb, c: (c, 0, 0))
    st_spec = pl.BlockSpec((None, SSM_NBLK, Bb, 2 * SSM_BLK_STATE), lambda bb, c: (l, 0, bb, 0))
    r_spec = pl.BlockSpec((None, Bb, RET_HEADS, RET_DK, RET_DV), lambda bb, c: (l, bb, 0, 0, 0))
    x_spec = pl.BlockSpec((Bb, T, D), lambda bb, c: (bb, c, 0)) if x_batch_major else act
    in_specs = [
        x_spec, lw((1, D)), lw((D, IN_COLS)),
        lw((SSM_NBLK, SSM_BLK_CH, 2 * SSM_BLK_STATE)), lw((SSM_NBLK, 2 * SSM_BLK_STATE, SSM_BLK_CH)),
        lw((SSM_NBLK, SUBLANES, SSM_BLK_STATE)), lw((SSM_NBLK, SUBLANES, SSM_BLK_STATE)),
        lw((1, D_SSM)), lw((D_SSM, D_SSM)), lw((D_SSM, D)), lw((V_W, D)), lw((D, D)),
        rope, rope,
        cst((RET_HEADS, T, T)), cst((RET_HEADS, T, RET_DV)), cst((RET_HEADS, T, RET_DK)),
        pl.BlockSpec(memory_space=pltpu.SMEM),
    ]
    args = [x, W["norm_mix"], W["w_in"], W["wb"], W["wc"], W["are"], W["aim"], W["d"],
            W["w_glu"], W["w_ssm_out"], W["w_ret_out"], W["w_o"], cosf, sinf, intra, inner, tail, decay]
    if carried:
        in_specs += [st_spec, r_spec]
        args += list(carried_state)
    aliases = {}
    for k, prev in enumerate(threaded):
        in_specs.append(pl.BlockSpec(memory_space=pl.ANY))
        aliases[len(args)] = 1 + k
        args.append(prev)
    slab = lambda n: pltpu.VMEM((n, R, LANES), F32)
    scratch = [pltpu.VMEM((R, D), BF16), pltpu.VMEM((2, R, 2 * SSM_BLK_STATE), F32),
               pltpu.VMEM((R, D_SSM), F32), pltpu.VMEM((R, D), F32),
               slab(QK_W // LANES), slab(QK_W // LANES), slab(V_W // LANES), slab(V_W // LANES),
               pltpu.VMEM((R, V_W), F32), pltpu.VMEM((R, D), F32)]
    if x_batch_major:
        scratch.append(slab(D // LANES))
    return pl.pallas_call(
        functools.partial(_mix_kernel, x_batch_major=x_batch_major, carried=carried, n_threaded=len(threaded)),
        grid=(B // Bb, L // T), in_specs=in_specs, out_specs=[act, st_spec, r_spec],
        out_shape=[jax.ShapeDtypeStruct((L, B, D), F32),
                   jax.ShapeDtypeStruct((DEPTH, SSM_NBLK, B, 2 * SSM_BLK_STATE), F32),
                   jax.ShapeDtypeStruct((DEPTH, B, RET_HEADS, RET_DK, RET_DV), F32)],
        scratch_shapes=scratch, input_output_aliases=aliases,
        compiler_params=pltpu.CompilerParams(dimension_semantics=("arbitrary", "arbitrary"),
                                             vmem_limit_bytes=VMEM_LIMIT),
        name="mixer_fused")(*args)


def _ffn_kernel(*refs, carried, final, n_threaded, zero_fill):
    refs = list(refs)
    x_ref, nf_ref, wup_ref, cw_ref, cb_ref, wd_ref, nfin_ref = refs[:7]
    refs = refs[7:]
    if carried:
        cbuf_ref = refs[0]
        refs = refs[1:]
    refs = refs[n_threaded:]
    x2_ref, cout_ref = refs[:2]
    refs = refs[2:]
    if zero_fill:
        refs[0][...] = jnp.zeros_like(refs[0])
        refs = refs[1:]
    hbuf, abuf = refs[:2]
    refs = refs[2:]

    c = pl.program_id(1)
    T, Bb, D = x_ref.shape
    R = T * Bb

    @pl.when(c == 0)
    def _():
        if carried:
            cout_ref[...] = cbuf_ref[...]
        else:
            cout_ref[...] = jnp.zeros_like(cout_ref)

    x1 = x_ref[...].reshape(R, D)
    hbuf[...] = _rms(x1, nf_ref[...]).astype(BF16)
    cols = lambda j: [part * D_FF + j * FF_TILE for part in range(2)]
    up_proj = lambda j: [_dot(hbuf[...], wup_ref[:, col:col + FF_TILE]) for col in cols(j)]
    ups = up_proj(0)
    for j in range(FF_NT):
        ups_next = up_proj(j + 1) if j + 1 < FF_NT else None
        halves = []
        for col, up in zip(cols(j), ups):
            carry = cout_ref[:, :, col:col + FF_TILE].reshape((CONV_W - 1) * Bb, FF_TILE)
            ext = jnp.concatenate([carry, up], axis=0)
            hc = cb_ref[:, col:col + FF_TILE]
            for jj in range(CONV_W):
                hc = hc + cw_ref[jj:jj + 1, col:col + FF_TILE] * ext[jj * Bb:jj * Bb + R]
            cout_ref[:, :, col:col + FF_TILE] = ext[R:R + (CONV_W - 1) * Bb].reshape(CONV_W - 1, Bb, FF_TILE)
            halves.append(hc)
        val, gate = halves
        abuf[:, j * FF_TILE:(j + 1) * FF_TILE] = ((gate * jax.nn.sigmoid(gate)) * val).astype(BF16)
        ups = ups_next
    x2 = x1 + _dot(abuf[...], wd_ref[...])
    if final:
        ys, = refs
        y = _rms(x2, nfin_ref[...])
        for s in range(D // LANES):
            ys[s] = y[:, s * LANES:(s + 1) * LANES]
        for b in range(Bb):
            for s in range(D // LANES):
                x2_ref[b, :, s * LANES:(s + 1) * LANES] = ys[s, pl.ds(b, T, stride=Bb), :]
    else:
        x2_ref[...] = x2.reshape(T, Bb, D)


def _ffn_stage(l, x, W, norm_final, T, Bb, final, cbuf=None, threaded=(), zero_fill=None):
    L, B, D = x.shape
    R = T * Bb
    carried = cbuf is not None
    act = pl.BlockSpec((T, Bb, D), lambda bb, c: (c, bb, 0))
    lw = lambda shape: _const_spec((None,) + shape, lambda bb, c: (l,) + (0,) * len(shape))
    cb_spec = pl.BlockSpec((None, CONV_W - 1, Bb, 2 * D_FF), lambda bb, c: (l, 0, bb, 0))
    in_specs = [act, lw((1, D)), lw((D, 2 * D_FF)),
                lw((CONV_W, 2 * D_FF)), lw((1, 2 * D_FF)), lw((D_FF, D)),
                _const_spec((1, D), lambda bb, c: (0, 0))]
    args = [x, W["norm_ffn"], W["w_up"], W["conv_w"], W["conv_b"], W["w_down"], norm_final]
    if carried:
        in_specs.append(cb_spec)
        args.append(cbuf)
    aliases = {}
    for prev in threaded:
        in_specs.append(pl.BlockSpec(memory_space=pl.ANY))
        aliases[len(args)] = 1
        args.append(prev)
    scratch = [pltpu.VMEM((R, D), BF16), pltpu.VMEM((R, D_FF), BF16)]
    if final:
        scratch.append(pltpu.VMEM((D // LANES, R, LANES), F32))
        out_spec, out_shape = pl.BlockSpec((Bb, T, D), lambda bb, c: (bb, c, 0)), (B, L, D)
    else:
        out_spec, out_shape = act, (L, B, D)
    out_specs = [out_spec, cb_spec]
    out_shapes = [jax.ShapeDtypeStruct(out_shape, F32),
                  jax.ShapeDtypeStruct((DEPTH, CONV_W - 1, B, 2 * D_FF), F32)]
    nbb, nc = B // Bb, L // T
    if zero_fill is not None:
        per0 = (nbb * nc) // zero_fill[0]
        rows = zero_fill[1] // per0
        assert per0 * zero_fill[0] == nbb * nc and rows * per0 == zero_fill[1]
        out_specs.append(pl.BlockSpec(
            (None, rows) + tuple(zero_fill[2:]),
            lambda bb, c: ((bb * nc + c) // per0, (bb * nc + c) % per0) + (0,) * (len(zero_fill) - 2)))
        out_shapes.append(jax.ShapeDtypeStruct(tuple(zero_fill), F32))
    return pl.pallas_call(
        functools.partial(_ffn_kernel, carried=carried, final=final, n_threaded=len(threaded),
                          zero_fill=zero_fill is not None),
        grid=(nbb, nc), in_specs=in_specs, out_specs=out_specs, out_shape=out_shapes,
        scratch_shapes=scratch, input_output_aliases=aliases,
        compiler_params=pltpu.CompilerParams(dimension_semantics=("arbitrary", "arbitrary"),
                                             vmem_limit_bytes=VMEM_LIMIT),
        name="conv_ffn")(*args)


def _prep_weights(norm_mix, w_in, lam_re, lam_im, log_dt, b_re, b_im, c_re, c_im, d, w_glu, w_ssm_out,
                  w_ret_out, w_o, norm_ffn, w_up, conv_w, conv_b, w_down):
    depth = w_in.shape[0]
    ar, ai, bbr, bbi = _s5_prep(lam_re, lam_im, log_dt, b_re, b_im)
    ch_group = jnp.arange(SSM_BLK_CH) // SSM_GROUP
    st_group = jnp.arange(SSM_BLK_STATE) // SSM_STATE
    in_mask = (ch_group[:, None] == st_group[None, :]).astype(BF16)

    def in_mat(bb):
        m = jnp.tile(bb.reshape(depth, SSM_NBLK, SSM_BLK_CH, SSM_STATE), (1, 1, 1, SSM_BLK_GROUPS))
        return m * in_mask

    def out_mat(cc):
        m = jnp.transpose(cc, (0, 1, 3, 2)).reshape(depth, SSM_NBLK, SSM_BLK_STATE, SSM_GROUP)
        return jnp.tile(m, (1, 1, 1, SSM_BLK_GROUPS)) * in_mask.T

    lam_rows = lambda a: jnp.broadcast_to(
        a.reshape(depth, SSM_GROUPS, SSM_GROUP, SSM_STATE)[:, :, 0].reshape(depth, SSM_NBLK, 1, SSM_BLK_STATE),
        (depth, SSM_NBLK, SUBLANES, SSM_BLK_STATE))
    return {
        "norm_mix": norm_mix[:, None, :],
        "w_in": w_in.astype(BF16),
        "wb": jnp.concatenate([in_mat(bbr.astype(BF16)), in_mat(bbi.astype(BF16))], axis=-1),
        "wc": jnp.concatenate([out_mat(c_re.astype(BF16)), -out_mat(c_im.astype(BF16))], axis=2),
        "are": lam_rows(ar), "aim": lam_rows(ai),
        "d": d[:, None, :],
        "w_glu": w_glu.astype(BF16), "w_ssm_out": w_ssm_out.astype(BF16),
        "w_ret_out": w_ret_out.astype(BF16), "w_o": w_o.astype(BF16),
        "norm_ffn": norm_ffn[:, None, :],
        "w_up": w_up.astype(BF16),
        "conv_w": conv_w, "conv_b": conv_b[:, None, :],
        "w_down": w_down.astype(BF16),
    }


def _rope_tables(pos):
    half = RET_DK // 2
    inv = ROPE_BASE ** (-jnp.arange(half, dtype=F32) / half)
    ang = pos[:, None] * inv[None, :]
    cos = jnp.cos(ang)
    sin = jnp.sin(ang)
    cosf = jnp.concatenate([cos, cos], axis=-1)[:, None, :]
    sinf = jnp.concatenate([-sin, sin], axis=-1)[:, None, :]
    return cosf, sinf


def _ssm_state_in(re, im):
    depth, B = re.shape[:2]
    f = lambda a: jnp.transpose(a.reshape(depth, B, SSM_NBLK, SSM_BLK_STATE), (0, 2, 1, 3))
    return jnp.concatenate([f(re), f(im)], axis=-1)


def _ssm_state_out(st):
    depth, _, B, _ = st.shape
    f = lambda a: jnp.transpose(a, (0, 2, 1, 3)).reshape(depth, B, SSM_GROUPS, SSM_STATE)
    return f(st[..., :SSM_BLK_STATE]), f(st[..., SSM_BLK_STATE:])


def _run_group(x, pos, W, norm_final, mix_tile, ffn_tile, state=None, ret_buffer=None, fill=None):
    cosf, sinf = _rope_tables(pos)
    swap_tb = lambda a: jnp.transpose(a, (0, 2, 1, 3))
    carried_state, cbuf = None, None
    if state is not None:
        ssm_re, ssm_im, ret0, conv0 = state
        carried_state = (_ssm_state_in(ssm_re, ssm_im), ret0)
        cbuf = swap_tb(conv0)
    B = x.shape[0]
    if ret_buffer is None:
        ret_buffer = jnp.zeros((DEPTH, B, RET_HEADS, RET_DK, RET_DV), F32)
    mix_acc = (jnp.zeros((DEPTH, SSM_NBLK, B, 2 * SSM_BLK_STATE), F32), ret_buffer)
    conv_acc = (jnp.zeros((DEPTH, CONV_W - 1, B, 2 * D_FF), F32),)
    xt, filled = x, ()
    for l in range(DEPTH):
        xt, *mix_acc = _mix_stage(l, xt, cosf, sinf, W, *mix_tile, x_batch_major=(l == 0),
                                  carried_state=carried_state, threaded=tuple(mix_acc))
        tile, zero_fill = (fill[1], fill[0]) if (fill is not None and l == 0) else (ffn_tile, None)
        xt, *conv_acc = _ffn_stage(l, xt, W, norm_final[None, :], *tile, final=(l == DEPTH - 1),
                                   cbuf=cbuf, threaded=tuple(conv_acc), zero_fill=zero_fill)
        if zero_fill is not None:
            filled = (conv_acc.pop(),)
    st, ret = mix_acc
    sre, sim = _ssm_state_out(st)
    return (xt, sre, sim, ret, swap_tb(conv_acc[0])) + filled


def kernel(x_prompt, x_sample, state_ssm_re, state_ssm_im, state_ret, state_conv, norm_mix, w_in, ssm_lam_re, ssm_lam_im, ssm_log_dt, ssm_b_re, ssm_b_im, ssm_c_re, ssm_c_im, ssm_d, w_glu, w_ssm_out, w_ret_out, w_o, norm_ffn, w_up, conv_w, conv_b, w_down, norm_final):
    W = _prep_weights(norm_mix, w_in, ssm_lam_re, ssm_lam_im, ssm_log_dt, ssm_b_re, ssm_b_im, ssm_c_re,
                      ssm_c_im, ssm_d, w_glu, w_ssm_out, w_ret_out, w_o, norm_ffn, w_up, conv_w, conv_b, w_down)
    bp, seq, _ = x_prompt.shape
    bs, dseq, _ = x_sample.shape
    tp = min(64, seq)
    yp, rep, imp, retp, convp, ret_buffer = _run_group(
        x_prompt, jnp.arange(seq, dtype=F32), W, norm_final, mix_tile=(tp, bp), ffn_tile=(min(128, seq), bp),
        fill=(state_ret.shape, (tp, bp)))
    ys, res, ims, rets, convs = _run_group(
        x_sample, PAST_LEN + jnp.arange(dseq, dtype=F32), W, norm_final,
        mix_tile=(dseq, min(8, bs)), ffn_tile=(dseq, min(64, bs)),
        state=(state_ssm_re, state_ssm_im, state_ret, state_conv), ret_buffer=ret_buffer)
    return (yp, ys, rep, imp, retp, convp, res, ims, rets, convs)
```
